```python
import jax, jax.numpy as jnp
from jax import lax
import numpy as np

D_MODEL = 2048
BATCH = 8
SEQ = 2048
DEPTH = 1

HEAD_DIM = 128
N_MIX_HEADS = D_MODEL // HEAD_DIM
N_MEM_HEADS = 4
N_SGU_HEADS = (N_MIX_HEADS - N_MEM_HEADS) // 2
N_CONV_GROUPS = N_MIX_HEADS - N_MEM_HEADS - N_SGU_HEADS
D_SGU = N_SGU_HEADS * HEAD_DIM
D_CONV = N_CONV_GROUPS * HEAD_DIM
D_MEM = N_MEM_HEADS * HEAD_DIM
D_MIX = D_SGU + D_CONV + D_MEM
D_IN = 2 * D_SGU + 3 * D_CONV + D_MEM
CHUNK = 128
CONV_W = 3
N_MEM = 256
D_FF = 4 * D_MODEL
EPS = 1e-6

kernel_name = "hybrid_sgu_shortconv_memattn_block"


def rms_norm(x, g):
    xf = x.astype(jnp.float32)
    y = xf * lax.rsqrt(jnp.mean(xf * xf, axis=-1, keepdims=True) + EPS)
    return (y * g.astype(jnp.float32)).astype(x.dtype)


def layer_norm(x, g, b):
    xf = x.astype(jnp.float32)
    mu = jnp.mean(xf, axis=-1, keepdims=True)
    xc = xf - mu
    y = xc * lax.rsqrt(jnp.mean(xc * xc, axis=-1, keepdims=True) + EPS)
    return (y * g.astype(jnp.float32) + b.astype(jnp.float32)).astype(x.dtype)


def chunked_spatial_gating(u, v, ln_g, ln_b, w_s, b_s):
    bsz, s, _ = v.shape
    u = jax.nn.gelu(u)
    v = layer_norm(jax.nn.gelu(v), ln_g, ln_b)
    vc = v.reshape(bsz, s // CHUNK, CHUNK, N_SGU_HEADS, HEAD_DIM)
    causal = jnp.tril(jnp.ones((CHUNK, CHUNK), dtype=bool))
    w = jnp.where(causal[None], w_s, jnp.zeros_like(w_s))
    mixed = jnp.einsum('hts,bcshd->bcthd', w, vc) + b_s.T[:, :, None]
    return u * mixed.reshape(bsz, s, D_SGU)


def short_gated_conv(b_gate, c_gate, xin, conv_w):
    xc = c_gate * xin
    y = lax.conv_general_dilated(
        xc, conv_w[:, None, :].astype(xc.dtype),
        window_strides=(1,), padding=[(CONV_W - 1, 0)],
        dimension_numbers=('NWC', 'WIO', 'NWC'),
        feature_group_count=D_CONV)
    return b_gate * y


def memory_attention(q, mem_n, w_kv):
    bsz, s, _ = q.shape
    m = mem_n.shape[1]
    k, v = jnp.split(mem_n @ w_kv, 2, axis=-1)
    q = q.reshape(bsz, s, N_MEM_HEADS, HEAD_DIM) * (HEAD_DIM ** -0.5)
    k = k.reshape(bsz, m, N_MEM_HEADS, HEAD_DIM)
    v = v.reshape(bsz, m, N_MEM_HEADS, HEAD_DIM)
    scores = jnp.einsum('bshd,bmhd->bhsm', q, k).astype(jnp.float32)
    p = jax.nn.softmax(scores, axis=-1).astype(v.dtype)
    o = jnp.einsum('bhsm,bmhd->bshd', p, v)
    return o.reshape(bsz, s, D_MEM)


def setup_inputs(seed: int = 0) -> dict:
    key = jax.random.key(seed)
    ks = jax.random.split(key, 20)
    f32 = jnp.float32
    nrm = lambda k, shape, scale: jax.random.normal(k, shape, f32) * scale
    gain = lambda k, shape: 1.0 + 0.02 * jax.random.normal(k, shape, f32)
    return {
        "x": jax.random.normal(ks[0], (BATCH, SEQ, D_MODEL), f32),
        "mem": jax.random.normal(ks[1], (BATCH, N_MEM, D_MODEL), f32),
        "g_mix": gain(ks[2], (DEPTH, D_MODEL)),
        "w_in": nrm(ks[3], (DEPTH, D_MODEL, D_IN), D_MODEL ** -0.5),
        "ln_v_g": gain(ks[4], (DEPTH, D_SGU)),
        "ln_v_b": nrm(ks[5], (DEPTH, D_SGU), 0.02),
        "w_s": nrm(ks[6], (DEPTH, N_SGU_HEADS, CHUNK, CHUNK), CHUNK ** -0.5),
        "b_s": gain(ks[7], (DEPTH, N_SGU_HEADS, CHUNK)),
        "conv_w": nrm(ks[8], (DEPTH, CONV_W, D_CONV), CONV_W ** -0.5),
        "g_mem": gain(ks[9], (DEPTH, D_MODEL)),
        "w_kv": nrm(ks[10], (DEPTH, D_MODEL, 2 * D_MEM), D_MODEL ** -0.5),
        "g_head": gain(ks[11], (DEPTH, D_MIX)),
        "w_o": nrm(ks[12], (DEPTH, D_MIX, D_MODEL), D_MIX ** -0.5),
        "g_ffn": gain(ks[13], (DEPTH, D_MODEL)),
        "w_ffn1": nrm(ks[14], (DEPTH, D_MODEL, D_FF), D_MODEL ** -0.5),
        "w_ffn2": nrm(ks[15], (DEPTH, D_FF, D_MODEL), D_FF ** -0.5),
        "g_final": gain(ks[16], (D_MODEL,)),
    }


def reference(x, mem, g_mix, w_in, ln_v_g, ln_v_b, w_s, b_s, conv_w, g_mem,
              w_kv, g_head, w_o, g_ffn, w_ffn1, w_ffn2, g_final):
    bsz, s, _ = x.shape
    split_at = np.cumsum([D_SGU, D_SGU, D_CONV, D_CONV, D_CONV])
    for l in range(DEPTH):
        h = rms_norm(x, g_mix[l])
        proj = h @ w_in[l]
        u, v, b_gate, c_gate, xin, q = jnp.split(proj, split_at, axis=-1)
        a_out = chunked_spatial_gating(u, v, ln_v_g[l], ln_v_b[l], w_s[l], b_s[l])
        c_out = short_gated_conv(b_gate, c_gate, xin, conv_w[l])
        m_out = memory_attention(q, rms_norm(mem, g_mem[l]), w_kv[l])
        heads = jnp.concatenate([a_out, c_out, m_out], axis=-1)
        heads = rms_norm(heads.reshape(bsz, s, N_MIX_HEADS, HEAD_DIM),
                         jnp.ones((HEAD_DIM,), heads.dtype)).reshape(bsz, s, D_MIX)
        x = x + (heads * g_head[l]) @ w_o[l]
        f = rms_norm(x, g_ffn[l]) @ w_ffn1[l]
        x = x + jnp.square(jax.nn.relu(f)) @ w_ffn2[l]
    return rms_norm(x, g_final)
```

```python
import functools

import jax
import jax.numpy as jnp
from jax import lax
from jax.experimental import pallas as pl
from jax.experimental.pallas import tpu as pltpu

HEAD_DIM = 128
N_MEM_HEADS = 4
CHUNK = 128
CONV_W = 3
EPS = 1e-6

SEQ_TILE = 512
FFN_ROW_TILE = 512
FFN_COL_TILE = 1024
VMEM_LIMIT_BYTES = 56 * 1024 * 1024

F32 = jnp.float32
BF16 = jnp.bfloat16


def _rms(xf, g):
    return xf * lax.rsqrt(jnp.mean(xf * xf, axis=-1, keepdims=True) + EPS) * g


def _dot(a, b):
    return jnp.dot(a, b, preferred_element_type=F32)


def _kv_kernel(mem_ref, g_ref, w_ref, kv_ref):
    m = _rms(mem_ref[...], g_ref[...]).astype(BF16)
    kv_ref[...] = _dot(m, w_ref[...]).astype(BF16)


def _kv_call(mem, g_mem, w_kv):
    bsz, n_mem, d = mem.shape
    n_out = w_kv.shape[1]
    return pl.pallas_call(
        _kv_kernel,
        grid=(bsz,),
        in_specs=[
            pl.BlockSpec((None, n_mem, d), lambda b: (b, 0, 0)),
            pl.BlockSpec((1, d), lambda b: (0, 0)),
            pl.BlockSpec((d, n_out), lambda b: (0, 0)),
        ],
        out_specs=pl.BlockSpec((None, n_mem, n_out), lambda b: (b, 0, 0)),
        out_shape=jax.ShapeDtypeStruct((bsz, n_mem, n_out), BF16),
        compiler_params=pltpu.CompilerParams(
            dimension_semantics=("arbitrary",), vmem_limit_bytes=VMEM_LIMIT_BYTES),
        name="mem_kv",
    )(mem, g_mem, w_kv)


def _mixer_kernel(x_ref, kv_ref, g_mix_ref, w_in_ref, ln_g_ref, ln_b_ref, w_s_ref,
                  bias_ref, conv_w_ref, g_head_ref, w_o_ref, o_ref,
                  heads_ref, carry_ref, *, ts, d_sgu, d_conv, d_mem):
    s = pl.program_id(1)
    n_chunks = ts // CHUNK
    n_sgu_heads = d_sgu // HEAD_DIM

    x = x_ref[...]
    h = _rms(x, g_mix_ref[...]).astype(BF16)

    def store_head(a, r0, nrows, col):
        y = a * lax.rsqrt(jnp.mean(a * a, axis=-1, keepdims=True) + EPS)
        y = y * g_head_ref[:, col:col + HEAD_DIM]
        heads_ref[r0:r0 + nrows, col:col + HEAD_DIM] = y.astype(BF16)

    uv = _dot(h, w_in_ref[:, 0:2 * d_sgu])
    u = jax.nn.gelu(uv[:, 0:d_sgu])
    v = jax.nn.gelu(uv[:, d_sgu:2 * d_sgu])
    mu = jnp.mean(v, axis=-1, keepdims=True)
    vc = v - mu
    vn = vc * lax.rsqrt(jnp.mean(vc * vc, axis=-1, keepdims=True) + EPS)
    vn = (vn * ln_g_ref[...] + ln_b_ref[...]).astype(BF16)
    t_idx = lax.broadcasted_iota(jnp.int32, (CHUNK, CHUNK), 0)
    s_idx = lax.broadcasted_iota(jnp.int32, (CHUNK, CHUNK), 1)
    causal = s_idx <= t_idx
    for hd in range(n_sgu_heads):
        col = hd * HEAD_DIM
        w = jnp.where(causal, w_s_ref[hd], 0.0).astype(BF16)
        v_h = jnp.concatenate(
            [vn[c * CHUNK:(c + 1) * CHUNK, col:col + HEAD_DIM] for c in range(n_chunks)], axis=1)
        mixed = _dot(w, v_h)
        bias = bias_ref[:, col:col + HEAD_DIM]
        for c in range(n_chunks):
            r0 = c * CHUNK
            a = u[r0:r0 + CHUNK, col:col + HEAD_DIM] * (
                mixed[:, c * HEAD_DIM:(c + 1) * HEAD_DIM] + bias)
            store_head(a, r0, CHUNK, col)

    @pl.when(s == 0)
    def _():
        carry_ref[...] = jnp.zeros_like(carry_ref)

    o_conv = 2 * d_sgu
    pc = _dot(h, w_in_ref[:, o_conv:o_conv + 3 * d_conv])
    b_gate = pc[:, 0:d_conv]
    xc = pc[:, d_conv:2 * d_conv] * pc[:, 2 * d_conv:3 * d_conv]
    prev = carry_ref[...]
    rows = lax.broadcasted_iota(jnp.int32, (ts, d_conv), 0)
    x_m1 = jnp.where(rows == 0, prev[7:8, :], pltpu.roll(xc, 1, 0))
    x_m2 = jnp.where(rows == 0, prev[6:7, :],
                     jnp.where(rows == 1, prev[7:8, :], pltpu.roll(xc, 2, 0)))
    y = (conv_w_ref[0:1, :] * x_m2 + conv_w_ref[1:2, :] * x_m1 + conv_w_ref[2:3, :] * xc)
    c_out = b_gate * y
    carry_ref[...] = xc[ts - 8:ts, :]
    for hd in range(d_conv // HEAD_DIM):
        store_head(c_out[:, hd * HEAD_DIM:(hd + 1) * HEAD_DIM], 0, ts, d_sgu + hd * HEAD_DIM)

    o_q = o_conv + 3 * d_conv
    q = _dot(h, w_in_ref[:, o_q:o_q + d_mem]) * (HEAD_DIM ** -0.5)
    for hd in range(N_MEM_HEADS):
        col = hd * HEAD_DIM
        q_h = q[:, col:col + HEAD_DIM].astype(BF16)
        k_h = kv_ref[:, col:col + HEAD_DIM]
        v_h = kv_ref[:, d_mem + col:d_mem + col + HEAD_DIM]
        sc = lax.dot_general(q_h, k_h, (((1,), (1,)), ((), ())), preferred_element_type=F32)
        e = jnp.exp(sc - jnp.max(sc, axis=-1, keepdims=True))
        denom = jnp.sum(e, axis=-1, keepdims=True)
        m_out = _dot(e.astype(BF16), v_h) / denom
        store_head(m_out, 0, ts, d_sgu + d_conv + col)

    o_ref[...] = x + _dot(heads_ref[...], w_o_ref[...])


def _mixer_call(x, kv, g_mix, w_in, ln_g, ln_b, w_s, bias_full, conv_w, g_head, w_o):
    bsz, seq, d = x.shape
    d_in = w_in.shape[1]
    d_mix = w_o.shape[0]
    d_sgu = ln_g.shape[1]
    d_conv = conv_w.shape[1]
    d_mem = kv.shape[2] // 2
    n_mem = kv.shape[1]
    ts = SEQ_TILE
    assert seq % ts == 0 and ts % CHUNK == 0
    assert d_in == 2 * d_sgu + 3 * d_conv + d_mem and d_mix == d_sgu + d_conv + d_mem
    const = lambda b, s: (0, 0)
    resident = pl.Buffered(1)
    kern = functools.partial(_mixer_kernel, ts=ts, d_sgu=d_sgu, d_conv=d_conv, d_mem=d_mem)
    return pl.pallas_call(
        kern,
        grid=(bsz, seq // ts),
        in_specs=[
            pl.BlockSpec((None, ts, d), lambda b, s: (b, s, 0)),
            pl.BlockSpec((None, n_mem, 2 * d_mem), lambda b, s: (b, 0, 0)),
            pl.BlockSpec((1, d), const),
            pl.BlockSpec((d, d_in), const, pipeline_mode=resident),
            pl.BlockSpec((1, d_sgu), const),
            pl.BlockSpec((1, d_sgu), const),
            pl.BlockSpec(w_s.shape, lambda b, s: (0, 0, 0)),
            pl.BlockSpec((CHUNK, d_sgu), const),
            pl.BlockSpec((CONV_W, d_conv), const),
            pl.BlockSpec((1, d_mix), const),
            pl.BlockSpec((d_mix, d), const, pipeline_mode=resident),
        ],
        out_specs=pl.BlockSpec((None, ts, d), lambda b, s: (b, s, 0)),
        out_shape=jax.ShapeDtypeStruct((bsz, seq, d), F32),
        scratch_shapes=[
            pltpu.VMEM((ts, d_mix), BF16),
            pltpu.VMEM((8, d_conv), F32),
        ],
        compiler_params=pltpu.CompilerParams(
            dimension_semantics=("arbitrary", "arbitrary"), vmem_limit_bytes=VMEM_LIMIT_BYTES),
        name="mixer",
    )(x, kv, g_mix, w_in, ln_g, ln_b, w_s, bias_full, conv_w, g_head, w_o)


def _ffn_kernel(x_ref, g_ffn_ref, w1_ref, w2_ref, g_final_ref, o_ref, hn_ref, acc_ref):
    j = pl.program_id(1)

    @pl.when(j == 0)
    def _():
        hn_ref[...] = _rms(x_ref[...], g_ffn_ref[...]).astype(BF16)
        acc_ref[...] = jnp.zeros_like(acc_ref)

    f = jnp.maximum(_dot(hn_ref[...], w1_ref[...]), 0.0)
    acc_ref[...] += _dot((f * f).astype(BF16), w2_ref[...])

    @pl.when(j == pl.num_programs(1) - 1)
    def _():
        o_ref[...] = _rms(x_ref[...] + acc_ref[...], g_final_ref[...])


def _ffn_call(x, g_ffn, w1, w2, g_final):
    t, d = x.shape
    d_ff = w1.shape[1]
    tm, tf = FFN_ROW_TILE, FFN_COL_TILE
    assert t % tm == 0 and d_ff % tf == 0
    return pl.pallas_call(
        _ffn_kernel,
        grid=(t // tm, d_ff // tf),
        in_specs=[
            pl.BlockSpec((tm, d), lambda i, j: (i, 0)),
            pl.BlockSpec((1, d), lambda i, j: (0, 0)),
            pl.BlockSpec((d, tf), lambda i, j: (0, j)),
            pl.BlockSpec((tf, d), lambda i, j: (j, 0)),
            pl.BlockSpec((1, d), lambda i, j: (0, 0)),
        ],
        out_specs=pl.BlockSpec((tm, d), lambda i, j: (i, 0)),
        out_shape=jax.ShapeDtypeStruct((t, d), F32),
        scratch_shapes=[pltpu.VMEM((tm, d), BF16), pltpu.VMEM((tm, d), F32)],
        compiler_params=pltpu.CompilerParams(
            dimension_semantics=("arbitrary", "arbitrary"), vmem_limit_bytes=VMEM_LIMIT_BYTES),
        name="ffn",
    )(x, g_ffn, w1, w2, g_final)


def kernel(x, mem, g_mix, w_in, ln_v_g, ln_v_b, w_s, b_s, conv_w, g_mem, w_kv, g_head, w_o,
           g_ffn, w_ffn1, w_ffn2, g_final):
    bsz, seq, d = x.shape
    assert w_in.shape[0] == 1, "the FFN call fuses the final norm, so exactly one layer is supported"
    kv = _kv_call(mem, g_mem[0][None, :], w_kv[0].astype(BF16))
    bias_full = jnp.repeat(b_s[0].T, HEAD_DIM, axis=1)
    x1 = _mixer_call(x, kv, g_mix[0][None, :], w_in[0].astype(BF16), ln_v_g[0][None, :],
                     ln_v_b[0][None, :], w_s[0], bias_full, conv_w[0], g_head[0][None, :],
                     w_o[0].astype(BF16))
    y = _ffn_call(x1.reshape(bsz * seq, d), g_ffn[0][None, :], w_ffn1[0].astype(BF16),
                  w_ffn2[0].astype(BF16), g_final[None, :])
    return y.reshape(bsz, seq, d)
```

```python
import functools

import jax
import jax.numpy as jnp
from jax import lax
from jax.experimental import pallas as pl
from jax.experimental.pallas import tpu as pltpu

HEAD_DIM = 128
N_MEM_HEADS = 4
CHUNK = 128
CONV_W = 3
EPS = 1e-6

SEQ_TILE = 512
FFN_ROW_TILE = 512
FFN_COL_TILE = 1024
VMEM_LIMIT_BYTES = 56 * 1024 * 1024

F32 = jnp.float32
BF16 = jnp.bfloat16


def _rms(xf, g):
    return xf * lax.rsqrt(jnp.mean(xf * xf, axis=-1, keepdims=True) + EPS) * g


_dot = functools.partial(jnp.dot, preferred_element_type=F32)


def _kv_kernel(mem_ref, g_ref, w_ref, kv_ref):
    m = _rms(mem_ref[...], g_ref[...]).astype(BF16)
    kv_ref[...] = _dot(m, w_ref[...]).astype(BF16)


def _kv_call(mem, g_mem, w_kv):
    bsz, n_mem, d = mem.shape
    n_out = w_kv.shape[1]
    return pl.pallas_call(
        _kv_kernel,
        grid=(bsz,),
        in_specs=[
            pl.BlockSpec((None, n_mem, d), lambda b: (b, 0, 0)),
            pl.BlockSpec((1, d), lambda b: (0, 0)),
            pl.BlockSpec((d, n_out), lambda b: (0, 0)),
        ],
        out_specs=pl.BlockSpec((None, n_mem, n_out), lambda b: (b, 0, 0)),
        out_shape=jax.ShapeDtypeStruct((bsz, n_mem, n_out), BF16),
        compiler_params=pltpu.CompilerParams(
            dimension_semantics=("arbitrary",), vmem_limit_bytes=VMEM_LIMIT_BYTES),
        name="mem_kv",
    )(mem, g_mem, w_kv)


def _mixer_kernel(x_ref, kv_ref, g_mix_ref, w_in_ref, ln_g_ref, ln_b_ref, w_s_ref,
                  bias_ref, conv_w_ref, g_head_ref, w_o_ref, o_ref,
                  heads_a_ref, heads_b_ref, heads_c_ref, carry_ref, *, ts, d_sgu, d_conv, d_mem):
    s = pl.program_id(1)
    n_chunks = ts // CHUNK
    n_sgu_heads = d_sgu // HEAD_DIM

    @pl.when(s == 0)
    def _():
        carry_ref[...] = jnp.zeros_like(carry_ref)

    x = x_ref[...]
    h = _rms(x, g_mix_ref[...]).astype(BF16)
    o_conv = 2 * d_sgu
    o_q = o_conv + 3 * d_conv

    def head_norm(a, col):
        y = a * lax.rsqrt(jnp.mean(a * a, axis=-1, keepdims=True) + EPS)
        return (y * g_head_ref[:, col:col + HEAD_DIM]).astype(BF16)

    uv = _dot(h, w_in_ref[:, 0:2 * d_sgu])
    q = _dot(h, w_in_ref[:, o_q:o_q + d_mem]) * (HEAD_DIM ** -0.5)
    cx = _dot(h, w_in_ref[:, o_conv + d_conv:o_conv + 3 * d_conv])

    u = jax.nn.gelu(uv[:, 0:d_sgu])
    v = jax.nn.gelu(uv[:, d_sgu:2 * d_sgu])
    mu = jnp.mean(v, axis=-1, keepdims=True)
    vc = v - mu
    vn = vc * lax.rsqrt(jnp.mean(vc * vc, axis=-1, keepdims=True) + EPS)
    vn = (vn * ln_g_ref[...] + ln_b_ref[...]).astype(BF16)

    scores = []
    for hd in range(N_MEM_HEADS):
        col = hd * HEAD_DIM
        q_h = q[:, col:col + HEAD_DIM].astype(BF16)
        k_h = kv_ref[:, col:col + HEAD_DIM]
        scores.append(lax.dot_general(q_h, k_h, (((1,), (1,)), ((), ())),
                                      preferred_element_type=F32))
    b_gate = _dot(h, w_in_ref[:, o_conv:o_conv + d_conv])

    t_idx = lax.broadcasted_iota(jnp.int32, (CHUNK, CHUNK), 0)
    s_idx = lax.broadcasted_iota(jnp.int32, (CHUNK, CHUNK), 1)
    causal = s_idx <= t_idx
    mixed = []
    for hd in range(n_sgu_heads):
        col = hd * HEAD_DIM
        w = jnp.where(causal, w_s_ref[hd], 0.0).astype(BF16)
        v_h = jnp.concatenate(
            [vn[c * CHUNK:(c + 1) * CHUNK, col:col + HEAD_DIM] for c in range(n_chunks)], axis=1)
        mixed.append(_dot(w, v_h))

    pv = []
    for hd in range(N_MEM_HEADS):
        col = hd * HEAD_DIM
        sc = scores[hd]
        e = jnp.exp(sc - jnp.max(sc, axis=-1, keepdims=True))
        denom = jnp.sum(e, axis=-1, keepdims=True)
        v_h = kv_ref[:, d_mem + col:d_mem + col + HEAD_DIM]
        pv.append((_dot(e.astype(BF16), v_h), denom))

    xc = cx[:, 0:d_conv] * cx[:, d_conv:2 * d_conv]
    prev = carry_ref[...]
    rows = lax.broadcasted_iota(jnp.int32, (ts, d_conv), 0)
    x_m1 = jnp.where(rows == 0, prev[7:8, :], pltpu.roll(xc, 1, 0))
    x_m2 = jnp.where(rows == 0, prev[6:7, :],
                     jnp.where(rows == 1, prev[7:8, :], pltpu.roll(xc, 2, 0)))
    y = (conv_w_ref[0:1, :] * x_m2 + conv_w_ref[1:2, :] * x_m1 + conv_w_ref[2:3, :] * xc)
    c_out = b_gate * y
    carry_ref[...] = xc[ts - 8:ts, :]
    for hd in range(d_conv // HEAD_DIM):
        col = hd * HEAD_DIM
        heads_b_ref[:, col:col + HEAD_DIM] = head_norm(c_out[:, col:col + HEAD_DIM], d_sgu + col)
    out = x + _dot(heads_b_ref[...], w_o_ref[d_sgu:d_sgu + d_conv, :])

    for hd in range(n_sgu_heads):
        col = hd * HEAD_DIM
        bias = bias_ref[:, col:col + HEAD_DIM]
        for c in range(n_chunks):
            r0 = c * CHUNK
            a = u[r0:r0 + CHUNK, col:col + HEAD_DIM] * (
                mixed[hd][:, c * HEAD_DIM:(c + 1) * HEAD_DIM] + bias)
            heads_a_ref[r0:r0 + CHUNK, col:col + HEAD_DIM] = head_norm(a, col)
    out = out + _dot(heads_a_ref[...], w_o_ref[0:d_sgu, :])

    for hd in range(N_MEM_HEADS):
        col = hd * HEAD_DIM
        o_h, denom = pv[hd]
        heads_c_ref[:, col:col + HEAD_DIM] = head_norm(o_h / denom, d_sgu + d_conv + col)
    out = out + _dot(heads_c_ref[...], w_o_ref[d_sgu + d_conv:d_sgu + d_conv + d_mem, :])
    o_ref[...] = out


def _mixer_call(x, kv, g_mix, w_in, ln_g, ln_b, w_s, bias_full, conv_w, g_head, w_o):
    bsz, seq, d = x.shape
    d_in = w_in.shape[1]
    d_mix = w_o.shape[0]
    d_sgu = ln_g.shape[1]
    d_conv = conv_w.shape[1]
    d_mem = kv.shape[2] // 2
    n_mem = kv.shape[1]
    ts = SEQ_TILE
    assert seq % ts == 0 and ts % CHUNK == 0
    assert d_in == 2 * d_sgu + 3 * d_conv + d_mem and d_mix == d_sgu + d_conv + d_mem
    const = lambda b, s: (0, 0)
    resident = pl.Buffered(1)
    kern = functools.partial(_mixer_kernel, ts=ts, d_sgu=d_sgu, d_conv=d_conv, d_mem=d_mem)
    return pl.pallas_call(
        kern,
        grid=(bsz, seq // ts),
        in_specs=[
            pl.BlockSpec((None, ts, d), lambda b, s: (b, s, 0)),
            pl.BlockSpec((None, n_mem, 2 * d_mem), lambda b, s: (b, 0, 0)),
            pl.BlockSpec((1, d), const),
            pl.BlockSpec((d, d_in), const, pipeline_mode=resident),
            pl.BlockSpec((1, d_sgu), const),
            pl.BlockSpec((1, d_sgu), const),
            pl.BlockSpec(w_s.shape, lambda b, s: (0, 0, 0)),
            pl.BlockSpec((CHUNK, d_sgu), const),
            pl.BlockSpec((CONV_W, d_conv), const),
            pl.BlockSpec((1, d_mix), const),
            pl.BlockSpec((d_mix, d), const, pipeline_mode=resident),
        ],
        out_specs=pl.BlockSpec((None, ts, d), lambda b, s: (b, s, 0)),
        out_shape=jax.ShapeDtypeStruct((bsz, seq, d), F32),
        scratch_shapes=[
            pltpu.VMEM((ts, d_sgu), BF16),
            pltpu.VMEM((ts, d_conv), BF16),
            pltpu.VMEM((ts, d_mem), BF16),
            pltpu.VMEM((8, d_conv), F32),
        ],
        compiler_params=pltpu.CompilerParams(
            dimension_semantics=("arbitrary", "arbitrary"), vmem_limit_bytes=VMEM_LIMIT_BYTES),
        name="mixer",
    )(x, kv, g_mix, w_in, ln_g, ln_b, w_s, bias_full, conv_w, g_head, w_o)


def _ffn_kernel(x_ref, g_ffn_ref, w1_ref, w2_ref, g_final_ref, o_ref, hn_ref, acc_ref):
    j = pl.program_id(1)

    @pl.when(j == 0)
    def _():
        hn_ref[...] = _rms(x_ref[...], g_ffn_ref[...]).astype(BF16)
        acc_ref[...] = jnp.zeros_like(acc_ref)

    f = jnp.maximum(_dot(hn_ref[...], w1_ref[...]), 0.0)
    acc_ref[...] += _dot((f * f).astype(BF16), w2_ref[...])

    @pl.when(j == pl.num_programs(1) - 1)
    def _():
        o_ref[...] = _rms(x_ref[...] + acc_ref[...], g_final_ref[...])


def _ffn_call(x, g_ffn, w1, w2, g_final):
    t, d = x.shape
    d_ff = w1.shape[1]
    tm, tf = FFN_ROW_TILE, FFN_COL_TILE
    assert t % tm == 0 and d_ff % tf == 0
    return pl.pallas_call(
        _ffn_kernel,
        grid=(t // tm, d_ff // tf),
        in_specs=[
            pl.BlockSpec((tm, d), lambda i, j: (i, 0)),
            pl.BlockSpec((1, d), lambda i, j: (0, 0)),
            pl.BlockSpec((d, tf), lambda i, j: (0, j)),
            pl.BlockSpec((tf, d), lambda i, j: (j, 0)),
            pl.BlockSpec((1, d), lambda i, j: (0, 0)),
        ],
        out_specs=pl.BlockSpec((tm, d), lambda i, j: (i, 0)),
        out_shape=jax.ShapeDtypeStruct((t, d), F32),
        scratch_shapes=[pltpu.VMEM((tm, d), BF16), pltpu.VMEM((tm, d), F32)],
        compiler_params=pltpu.CompilerParams(
            dimension_semantics=("arbitrary", "arbitrary"), vmem_limit_bytes=VMEM_LIMIT_BYTES),
        name="ffn",
    )(x, g_ffn, w1, w2, g_final)


def kernel(x, mem, g_mix, w_in, ln_v_g, ln_v_b, w_s, b_s, conv_w, g_mem, w_kv, g_head, w_o,
           g_ffn, w_ffn1, w_ffn2, g_final):
    bsz, seq, d = x.shape
    assert w_in.shape[0] == 1, "the FFN call fuses the final norm, so exactly one layer is supported"
    kv = _kv_call(mem, g_mem[0][None, :], w_kv[0].astype(BF16))
    bias_full = jnp.repeat(b_s[0].T, HEAD_DIM, axis=1)
    x1 = _mixer_call(x, kv, g_mix[0][None, :], w_in[0].astype(BF16), ln_v_g[0][None, :],
                     ln_v_b[0][None, :], w_s[0], bias_full, conv_w[0], g_head[0][None, :],
                     w_o[0].astype(BF16))
    y = _ffn_call(x1.reshape(bsz * seq, d), g_ffn[0][None, :], w_ffn1[0].astype(BF16),
                  w_ffn2[0].astype(BF16), g_final[None, :])
    return y.reshape(bsz, seq, d)
```

```python
import functools

import jax
import jax.numpy as jnp
from jax import lax
from jax.experimental import pallas as pl
from jax.experimental.pallas import tpu as pltpu

HEAD_DIM = 128
N_MEM_HEADS = 4
CHUNK = 128
CONV_W = 3
EPS = 1e-6

SEQ_TILE = 256
FFN_ROW_TILE = 512
FFN_COL_TILE = 1024
VMEM_LIMIT_BYTES = 56 * 1024 * 1024

F32 = jnp.float32
BF16 = jnp.bfloat16


def _rms(xf, g):
    return xf * lax.rsqrt(jnp.mean(xf * xf, axis=-1, keepdims=True) + EPS) * g


_dot = functools.partial(jnp.dot, preferred_element_type=F32)


def _kv_kernel(mem_ref, g_ref, w_ref, kv_ref):
    m = _rms(mem_ref[...], g_ref[...]).astype(BF16)
    kv_ref[...] = _dot(m, w_ref[...]).astype(BF16)


def _kv_call(mem, g_mem, w_kv):
    bsz, n_mem, d = mem.shape
    n_out = w_kv.shape[1]
    return pl.pallas_call(
        _kv_kernel,
        grid=(bsz,),
        in_specs=[
            pl.BlockSpec((None, n_mem, d), lambda b: (b, 0, 0)),
            pl.BlockSpec((1, d), lambda b: (0, 0)),
            pl.BlockSpec((d, n_out), lambda b: (0, 0)),
        ],
        out_specs=pl.BlockSpec((None, n_mem, n_out), lambda b: (b, 0, 0)),
        out_shape=jax.ShapeDtypeStruct((bsz, n_mem, n_out), BF16),
        compiler_params=pltpu.CompilerParams(
            dimension_semantics=("arbitrary",), vmem_limit_bytes=VMEM_LIMIT_BYTES),
        name="mem_kv",
    )(mem, g_mem, w_kv)


def _mixer_kernel(x_ref, kv_ref, g_mix_ref, w_in_ref, ln_g_ref, ln_b_ref, w_s_ref,
                  bias_ref, conv_w_ref, g_head_ref, w_o_ref, w1_ref, w2_ref,
                  o_ref, w1b_ref, w2b_ref,
                  heads_a_ref, heads_b_ref, heads_c_ref, carry_ref, *, ts, d_sgu, d_conv, d_mem):
    s = pl.program_id(1)
    n_chunks = ts // CHUNK
    n_sgu_heads = d_sgu // HEAD_DIM

    @pl.when(s == 0)
    def _():
        carry_ref[...] = jnp.zeros_like(carry_ref)

    x = x_ref[...]
    h = _rms(x, g_mix_ref[...]).astype(BF16)
    o_conv = 2 * d_sgu
    o_q = o_conv + 3 * d_conv

    def head_norm(a, col):
        y = a * lax.rsqrt(jnp.mean(a * a, axis=-1, keepdims=True) + EPS)
        return (y * g_head_ref[:, col:col + HEAD_DIM]).astype(BF16)

    uv = _dot(h, w_in_ref[:, 0:2 * d_sgu])
    q = _dot(h, w_in_ref[:, o_q:o_q + d_mem]) * (HEAD_DIM ** -0.5)
    cx = _dot(h, w_in_ref[:, o_conv + d_conv:o_conv + 3 * d_conv])

    u = jax.nn.gelu(uv[:, 0:d_sgu])
    v = jax.nn.gelu(uv[:, d_sgu:2 * d_sgu])
    mu = jnp.mean(v, axis=-1, keepdims=True)
    vc = v - mu
    vn = vc * lax.rsqrt(jnp.mean(vc * vc, axis=-1, keepdims=True) + EPS)
    vn = (vn * ln_g_ref[...] + ln_b_ref[...]).astype(BF16)

    scores = []
    for hd in range(N_MEM_HEADS):
        col = hd * HEAD_DIM
        q_h = q[:, col:col + HEAD_DIM].astype(BF16)
        k_h = kv_ref[:, col:col + HEAD_DIM]
        scores.append(lax.dot_general(q_h, k_h, (((1,), (1,)), ((), ())),
                                      preferred_element_type=F32))
    b_gate = _dot(h, w_in_ref[:, o_conv:o_conv + d_conv])

    w1b_ref[...] = w1_ref[...].astype(BF16)
    w2b_ref[...] = w2_ref[...].astype(BF16)

    t_idx = lax.broadcasted_iota(jnp.int32, (CHUNK, CHUNK), 0)
    s_idx = lax.broadcasted_iota(jnp.int32, (CHUNK, CHUNK), 1)
    causal = s_idx <= t_idx
    mixed = []
    for hd in range(n_sgu_heads):
        col = hd * HEAD_DIM
        w = jnp.where(causal, w_s_ref[hd], 0.0).astype(BF16)
        v_h = jnp.concatenate(
            [vn[c * CHUNK:(c + 1) * CHUNK, col:col + HEAD_DIM] for c in range(n_chunks)], axis=1)
        mixed.append(_dot(w, v_h))

    pv = []
    for hd in range(N_MEM_HEADS):
        col = hd * HEAD_DIM
        sc = scores[hd]
        e = jnp.exp(sc - jnp.max(sc, axis=-1, keepdims=True))
        denom = jnp.sum(e, axis=-1, keepdims=True)
        v_h = kv_ref[:, d_mem + col:d_mem + col + HEAD_DIM]
        pv.append((_dot(e.astype(BF16), v_h), denom))

    xc = cx[:, 0:d_conv] * cx[:, d_conv:2 * d_conv]
    prev = carry_ref[...]
    rows = lax.broadcasted_iota(jnp.int32, (ts, d_conv), 0)
    x_m1 = jnp.where(rows == 0, prev[7:8, :], pltpu.roll(xc, 1, 0))
    x_m2 = jnp.where(rows == 0, prev[6:7, :],
                     jnp.where(rows == 1, prev[7:8, :], pltpu.roll(xc, 2, 0)))
    y = (conv_w_ref[0:1, :] * x_m2 + conv_w_ref[1:2, :] * x_m1 + conv_w_ref[2:3, :] * xc)
    c_out = b_gate * y
    carry_ref[...] = xc[ts - 8:ts, :]
    for hd in range(d_conv // HEAD_DIM):
        col = hd * HEAD_DIM
        heads_b_ref[:, col:col + HEAD_DIM] = head_norm(c_out[:, col:col + HEAD_DIM], d_sgu + col)
    out = x + _dot(heads_b_ref[...], w_o_ref[d_sgu:d_sgu + d_conv, :])

    for hd in range(n_sgu_heads):
        col = hd * HEAD_DIM
        bias = bias_ref[:, col:col + HEAD_DIM]
        for c in range(n_chunks):
            r0 = c * CHUNK
            a = u[r0:r0 + CHUNK, col:col + HEAD_DIM] * (
                mixed[hd][:, c * HEAD_DIM:(c + 1) * HEAD_DIM] + bias)
            heads_a_ref[r0:r0 + CHUNK, col:col + HEAD_DIM] = head_norm(a, col)
    out = out + _dot(heads_a_ref[...], w_o_ref[0:d_sgu, :])

    for hd in range(N_MEM_HEADS):
        col = hd * HEAD_DIM
        o_h, denom = pv[hd]
        heads_c_ref[:, col:col + HEAD_DIM] = head_norm(o_h / denom, d_sgu + d_conv + col)
    out = out + _dot(heads_c_ref[...], w_o_ref[d_sgu + d_conv:d_sgu + d_conv + d_mem, :])
    o_ref[...] = out


def _mixer_call(x, kv, g_mix, w_in, ln_g, ln_b, w_s, bias_full, conv_w, g_head, w_o, w1, w2):
    bsz, seq, d = x.shape
    d_in = w_in.shape[1]
    d_mix = w_o.shape[0]
    d_sgu = ln_g.shape[1]
    d_conv = conv_w.shape[1]
    d_mem = kv.shape[2] // 2
    n_mem = kv.shape[1]
    ts = SEQ_TILE
    assert seq % ts == 0 and ts % CHUNK == 0
    assert d_in == 2 * d_sgu + 3 * d_conv + d_mem and d_mix == d_sgu + d_conv + d_mem
    n_s = seq // ts
    n_steps = bsz * n_s
    assert w1.shape[0] % (16 * n_steps) == 0 and w2.shape[0] % (16 * n_steps) == 0
    r1, r2 = w1.shape[0] // n_steps, w2.shape[0] // n_steps
    slab = lambda b, s: (b * n_s + s, 0)
    const = lambda b, s: (0, 0)
    resident = pl.Buffered(1)
    kern = functools.partial(_mixer_kernel, ts=ts, d_sgu=d_sgu, d_conv=d_conv, d_mem=d_mem)
    return pl.pallas_call(
        kern,
        grid=(bsz, seq // ts),
        in_specs=[
            pl.BlockSpec((None, ts, d), lambda b, s: (b, s, 0)),
            pl.BlockSpec((None, n_mem, 2 * d_mem), lambda b, s: (b, 0, 0)),
            pl.BlockSpec((1, d), const),
            pl.BlockSpec((d, d_in), const, pipeline_mode=resident),
            pl.BlockSpec((1, d_sgu), const),
            pl.BlockSpec((1, d_sgu), const),
            pl.BlockSpec(w_s.shape, lambda b, s: (0, 0, 0)),
            pl.BlockSpec((CHUNK, d_sgu), const),
            pl.BlockSpec((CONV_W, d_conv), const),
            pl.BlockSpec((1, d_mix), const),
            pl.BlockSpec((d_mix, d), const, pipeline_mode=resident),
            pl.BlockSpec((r1, w1.shape[1]), slab),
            pl.BlockSpec((r2, w2.shape[1]), slab),
        ],
        out_specs=[pl.BlockSpec((None, ts, d), lambda b, s: (b, s, 0)),
                   pl.BlockSpec((r1, w1.shape[1]), slab),
                   pl.BlockSpec((r2, w2.shape[1]), slab)],
        out_shape=[jax.ShapeDtypeStruct((bsz, seq, d), F32),
                   jax.ShapeDtypeStruct(w1.shape, BF16),
                   jax.ShapeDtypeStruct(w2.shape, BF16)],
        scratch_shapes=[
            pltpu.VMEM((ts, d_sgu), BF16),
            pltpu.VMEM((ts, d_conv), BF16),
            pltpu.VMEM((ts, d_mem), BF16),
            pltpu.VMEM((8, d_conv), F32),
        ],
        compiler_params=pltpu.CompilerParams(
            dimension_semantics=("arbitrary", "arbitrary"), vmem_limit_bytes=VMEM_LIMIT_BYTES),
        name="mixer",
    )(x, kv, g_mix, w_in, ln_g, ln_b, w_s, bias_full, conv_w, g_head, w_o, w1, w2)


def _ffn_kernel(x_ref, g_ffn_ref, w1_ref, w2_ref, g_final_ref, o_ref, hn_ref, acc_ref):
    j = pl.program_id(1)

    @pl.when(j == 0)
    def _():
        hn_ref[...] = _rms(x_ref[...], g_ffn_ref[...]).astype(BF16)
        acc_ref[...] = jnp.zeros_like(acc_ref)

    f = jnp.maximum(_dot(hn_ref[...], w1_ref[...]), 0.0)
    acc_ref[...] += _dot((f * f).astype(BF16), w2_ref[...])

    @pl.when(j == pl.num_programs(1) - 1)
    def _():
        o_ref[...] = _rms(x_ref[...] + acc_ref[...], g_final_ref[...])


def _ffn_call(x, g_ffn, w1, w2, g_final):
    t, d = x.shape
    d_ff = w1.shape[1]
    tm, tf = FFN_ROW_TILE, FFN_COL_TILE
    assert t % tm == 0 and d_ff % tf == 0
    return pl.pallas_call(
        _ffn_kernel,
        grid=(t // tm, d_ff // tf),
        in_specs=[
            pl.BlockSpec((tm, d), lambda i, j: (i, 0)),
            pl.BlockSpec((1, d), lambda i, j: (0, 0)),
            pl.BlockSpec((d, tf), lambda i, j: (0, j)),
            pl.BlockSpec((tf, d), lambda i, j: (j, 0)),
            pl.BlockSpec((1, d), lambda i, j: (0, 0)),
        ],
        out_specs=pl.BlockSpec((tm, d), lambda i, j: (i, 0)),
        out_shape=jax.ShapeDtypeStruct((t, d), F32),
        scratch_shapes=[pltpu.VMEM((tm, d), BF16), pltpu.VMEM((tm, d), F32)],
        compiler_params=pltpu.CompilerParams(
            dimension_semantics=("arbitrary", "arbitrary"), vmem_limit_bytes=VMEM_LIMIT_BYTES),
        name="ffn",
    )(x, g_ffn, w1, w2, g_final)


def kernel(x, mem, g_mix, w_in, ln_v_g, ln_v_b, w_s, b_s, conv_w, g_mem, w_kv, g_head, w_o,
           g_ffn, w_ffn1, w_ffn2, g_final):
    bsz, seq, d = x.shape
    assert w_in.shape[0] == 1, "the FFN call fuses the final norm, so exactly one layer is supported"
    kv = _kv_call(mem, g_mem[0][None, :], w_kv[0].astype(BF16))
    bias_full = jnp.repeat(b_s[0].T, HEAD_DIM, axis=1)
    x1, w1b, w2b = _mixer_call(x, kv, g_mix[0][None, :], w_in[0].astype(BF16), ln_v_g[0][None, :],
                               ln_v_b[0][None, :], w_s[0], bias_full, conv_w[0], g_head[0][None, :],
                               w_o[0].astype(BF16), w_ffn1[0], w_ffn2[0])
    y = _ffn_call(x1.reshape(bsz * seq, d), g_ffn[0][None, :], w1b, w2b, g_final[None, :])
    return y.reshape(bsz, seq, d)
```

```python
import functools

import jax
import jax.numpy as jnp
from jax import lax
from jax.experimental import pallas as pl
from jax.experimental.pallas import tpu as pltpu

HEAD_DIM = 128
N_MEM_HEADS = 4
CHUNK = 128
CONV_W = 3
EPS = 1e-6

SEQ_TILE = 256
FFN_ROW_TILE = 512
FFN_COL_TILE = 1024
VMEM_LIMIT_BYTES = 56 * 1024 * 1024

F32 = jnp.float32
BF16 = jnp.bfloat16


def _rms(xf, g):
    return xf * lax.rsqrt(jnp.mean(xf * xf, axis=-1, keepdims=True) + EPS) * g


_dot = functools.partial(jnp.dot, preferred_element_type=F32)


def _kv_kernel(mem_ref, g_ref, w_ref, w_in_ref, w_o_ref, kv_ref, w_in_b_ref, w_o_b_ref):
    m = _rms(mem_ref[...], g_ref[...]).astype(BF16)
    kv_ref[...] = _dot(m, w_ref[...].astype(BF16)).astype(BF16)
    w_in_b_ref[...] = w_in_ref[...].astype(BF16)
    w_o_b_ref[...] = w_o_ref[...].astype(BF16)


def _kv_call(mem, g_mem, w_kv, w_in, w_o):
    bsz, n_mem, d = mem.shape
    n_out = w_kv.shape[1]
    assert w_in.shape[0] % (16 * bsz) == 0 and w_o.shape[0] % (16 * bsz) == 0
    r_in, r_o = w_in.shape[0] // bsz, w_o.shape[0] // bsz
    slab = lambda b: (b, 0)
    return pl.pallas_call(
        _kv_kernel,
        grid=(bsz,),
        in_specs=[
            pl.BlockSpec((None, n_mem, d), lambda b: (b, 0, 0)),
            pl.BlockSpec((1, d), lambda b: (0, 0)),
            pl.BlockSpec((d, n_out), lambda b: (0, 0), pipeline_mode=pl.Buffered(1)),
            pl.BlockSpec((r_in, w_in.shape[1]), slab),
            pl.BlockSpec((r_o, w_o.shape[1]), slab),
        ],
        out_specs=[pl.BlockSpec((None, n_mem, n_out), lambda b: (b, 0, 0)),
                   pl.BlockSpec((r_in, w_in.shape[1]), slab),
                   pl.BlockSpec((r_o, w_o.shape[1]), slab)],
        out_shape=[jax.ShapeDtypeStruct((bsz, n_mem, n_out), BF16),
                   jax.ShapeDtypeStruct(w_in.shape, BF16),
                   jax.ShapeDtypeStruct(w_o.shape, BF16)],
        compiler_params=pltpu.CompilerParams(
            dimension_semantics=("arbitrary",), vmem_limit_bytes=VMEM_LIMIT_BYTES),
        name="mem_kv",
    )(mem, g_mem, w_kv, w_in, w_o)


def _mixer_kernel(x_ref, kv_ref, g_mix_ref, w_in_ref, ln_g_ref, ln_b_ref, w_s_ref,
                  bias_ref, conv_w_ref, g_head_ref, w_o_ref, w1_ref, w2_ref,
                  o_ref, w1b_ref, w2b_ref,
                  heads_a_ref, heads_b_ref, heads_c_ref, carry_ref, *, ts, d_sgu, d_conv, d_mem):
    s = pl.program_id(1)
    n_chunks = ts // CHUNK
    n_sgu_heads = d_sgu // HEAD_DIM

    @pl.when(s == 0)
    def _():
        carry_ref[...] = jnp.zeros_like(carry_ref)

    x = x_ref[...]
    h = _rms(x, g_mix_ref[...]).astype(BF16)
    o_conv = 2 * d_sgu
    o_q = o_conv + 3 * d_conv

    def head_norm(a, col):
        y = a * lax.rsqrt(jnp.mean(a * a, axis=-1, keepdims=True) + EPS)
        return (y * g_head_ref[:, col:col + HEAD_DIM]).astype(BF16)

    uv = _dot(h, w_in_ref[:, 0:2 * d_sgu])
    q = _dot(h, w_in_ref[:, o_q:o_q + d_mem]) * (HEAD_DIM ** -0.5)
    cx = _dot(h, w_in_ref[:, o_conv + d_conv:o_conv + 3 * d_conv])

    u = jax.nn.gelu(uv[:, 0:d_sgu])
    v = jax.nn.gelu(uv[:, d_sgu:2 * d_sgu])
    mu = jnp.mean(v, axis=-1, keepdims=True)
    vc = v - mu
    vn = vc * lax.rsqrt(jnp.mean(vc * vc, axis=-1, keepdims=True) + EPS)
    vn = (vn * ln_g_ref[...] + ln_b_ref[...]).astype(BF16)

    scores = []
    for hd in range(N_MEM_HEADS):
        col = hd * HEAD_DIM
        q_h = q[:, col:col + HEAD_DIM].astype(BF16)
        k_h = kv_ref[:, col:col + HEAD_DIM]
        scores.append(lax.dot_general(q_h, k_h, (((1,), (1,)), ((), ())),
                                      preferred_element_type=F32))
    b_gate = _dot(h, w_in_ref[:, o_conv:o_conv + d_conv])

    w1b_ref[...] = w1_ref[...].astype(BF16)
    w2b_ref[...] = w2_ref[...].astype(BF16)

    t_idx = lax.broadcasted_iota(jnp.int32, (CHUNK, CHUNK), 0)
    s_idx = lax.broadcasted_iota(jnp.int32, (CHUNK, CHUNK), 1)
    causal = s_idx <= t_idx
    mixed = []
    for hd in range(n_sgu_heads):
        col = hd * HEAD_DIM
        w = jnp.where(causal, w_s_ref[hd], 0.0).astype(BF16)
        v_h = jnp.concatenate(
            [vn[c * CHUNK:(c + 1) * CHUNK, col:col + HEAD_DIM] for c in range(n_chunks)], axis=1)
        mixed.append(_dot(w, v_h))

    pv = []
    for hd in range(N_MEM_HEADS):
        col = hd * HEAD_DIM
        sc = scores[hd]
        e = jnp.exp(sc - jnp.max(sc, axis=-1, keepdims=True))
        denom = jnp.sum(e, axis=-1, keepdims=True)
        v_h = kv_ref[:, d_mem + col:d_mem + col + HEAD_DIM]
        pv.append((_dot(e.astype(BF16), v_h), denom))

    xc = cx[:, 0:d_conv] * cx[:, d_conv:2 * d_conv]
    prev = carry_ref[...]
    rows = lax.broadcasted_iota(jnp.int32, (ts, d_conv), 0)
    x_m1 = jnp.where(rows == 0, prev[7:8, :], pltpu.roll(xc, 1, 0))
    x_m2 = jnp.where(rows == 0, prev[6:7, :],
                     jnp.where(rows == 1, prev[7:8, :], pltpu.roll(xc, 2, 0)))
    y = (conv_w_ref[0:1, :] * x_m2 + conv_w_ref[1:2, :] * x_m1 + conv_w_ref[2:3, :] * xc)
    c_out = b_gate * y
    carry_ref[...] = xc[ts - 8:ts, :]
    for hd in range(d_conv // HEAD_DIM):
        col = hd * HEAD_DIM
        heads_b_ref[:, col:col + HEAD_DIM] = head_norm(c_out[:, col:col + HEAD_DIM], d_sgu + col)
    out = x + _dot(heads_b_ref[...], w_o_ref[d_sgu:d_sgu + d_conv, :])

    for hd in range(n_sgu_heads):
        col = hd * HEAD_DIM
        bias = bias_ref[:, col:col + HEAD_DIM]
        for c in range(n_chunks):
            r0 = c * CHUNK
            a = u[r0:r0 + CHUNK, col:col + HEAD_DIM] * (
                mixed[hd][:, c * HEAD_DIM:(c + 1) * HEAD_DIM] + bias)
            heads_a_ref[r0:r0 + CHUNK, col:col + HEAD_DIM] = head_norm(a, col)
    out = out + _dot(heads_a_ref[...], w_o_ref[0:d_sgu, :])

    for hd in range(N_MEM_HEADS):
        col = hd * HEAD_DIM
        o_h, denom = pv[hd]
        heads_c_ref[:, col:col + HEAD_DIM] = head_norm(o_h / denom, d_sgu + d_conv + col)
    out = out + _dot(heads_c_ref[...], w_o_ref[d_sgu + d_conv:d_sgu + d_conv + d_mem, :])
    o_ref[...] = out


def _mixer_call(x, kv, g_mix, w_in, ln_g, ln_b, w_s, bias_full, conv_w, g_head, w_o, w1, w2):
    bsz, seq, d = x.shape
    d_in = w_in.shape[1]
    d_mix = w_o.shape[0]
    d_sgu = ln_g.shape[1]
    d_conv = conv_w.shape[1]
    d_mem = kv.shape[2] // 2
    n_mem = kv.shape[1]
    ts = SEQ_TILE
    assert seq % ts == 0 and ts % CHUNK == 0
    assert d_in == 2 * d_sgu + 3 * d_conv + d_mem and d_mix == d_sgu + d_conv + d_mem
    n_s = seq // ts
    n_steps = bsz * n_s
    assert w1.shape[0] % (16 * n_steps) == 0 and w2.shape[0] % (16 * n_steps) == 0
    r1, r2 = w1.shape[0] // n_steps, w2.shape[0] // n_steps
    slab = lambda b, s: (b * n_s + s, 0)
    const = lambda b, s: (0, 0)
    resident = pl.Buffered(1)
    kern = functools.partial(_mixer_kernel, ts=ts, d_sgu=d_sgu, d_conv=d_conv, d_mem=d_mem)
    return pl.pallas_call(
        kern,
        grid=(bsz, seq // ts),
        in_specs=[
            pl.BlockSpec((None, ts, d), lambda b, s: (b, s, 0)),
            pl.BlockSpec((None, n_mem, 2 * d_mem), lambda b, s: (b, 0, 0)),
            pl.BlockSpec((1, d), const),
            pl.BlockSpec((d, d_in), const, pipeline_mode=resident),
            pl.BlockSpec((1, d_sgu), const),
            pl.BlockSpec((1, d_sgu), const),
            pl.BlockSpec(w_s.shape, lambda b, s: (0, 0, 0)),
            pl.BlockSpec((CHUNK, d_sgu), const),
            pl.BlockSpec((CONV_W, d_conv), const),
            pl.BlockSpec((1, d_mix), const),
            pl.BlockSpec((d_mix, d), const, pipeline_mode=resident),
            pl.BlockSpec((r1, w1.shape[1]), slab),
            pl.BlockSpec((r2, w2.shape[1]), slab),
        ],
        out_specs=[pl.BlockSpec((None, ts, d), lambda b, s: (b, s, 0)),
                   pl.BlockSpec((r1, w1.shape[1]), slab),
                   pl.BlockSpec((r2, w2.shape[1]), slab)],
        out_shape=[jax.ShapeDtypeStruct((bsz, seq, d), F32),
                   jax.ShapeDtypeStruct(w1.shape, BF16),
                   jax.ShapeDtypeStruct(w2.shape, BF16)],
        scratch_shapes=[
            pltpu.VMEM((ts, d_sgu), BF16),
            pltpu.VMEM((ts, d_conv), BF16),
            pltpu.VMEM((ts, d_mem), BF16),
            pltpu.VMEM((8, d_conv), F32),
        ],
        compiler_params=pltpu.CompilerParams(
            dimension_semantics=("arbitrary", "arbitrary"), vmem_limit_bytes=VMEM_LIMIT_BYTES),
        name="mixer",
    )(x, kv, g_mix, w_in, ln_g, ln_b, w_s, bias_full, conv_w, g_head, w_o, w1, w2)


def _ffn_kernel(x_ref, g_ffn_ref, w1_ref, w2_ref, g_final_ref, o_ref, hn_ref, acc_ref):
    j = pl.program_id(1)

    @pl.when(j == 0)
    def _():
        hn_ref[...] = _rms(x_ref[...], g_ffn_ref[...]).astype(BF16)
        acc_ref[...] = jnp.zeros_like(acc_ref)

    f = jnp.maximum(_dot(hn_ref[...], w1_ref[...]), 0.0)
    acc_ref[...] += _dot((f * f).astype(BF16), w2_ref[...])

    @pl.when(j == pl.num_programs(1) - 1)
    def _():
        o_ref[...] = _rms(x_ref[...] + acc_ref[...], g_final_ref[...])


def _ffn_call(x, g_ffn, w1, w2, g_final):
    t, d = x.shape
    d_ff = w1.shape[1]
    tm, tf = FFN_ROW_TILE, FFN_COL_TILE
    assert t % tm == 0 and d_ff % tf == 0
    return pl.pallas_call(
        _ffn_kernel,
        grid=(t // tm, d_ff // tf),
        in_specs=[
            pl.BlockSpec((tm, d), lambda i, j: (i, 0)),
            pl.BlockSpec((1, d), lambda i, j: (0, 0)),
            pl.BlockSpec((d, tf), lambda i, j: (0, j)),
            pl.BlockSpec((tf, d), lambda i, j: (j, 0)),
            pl.BlockSpec((1, d), lambda i, j: (0, 0)),
        ],
        out_specs=pl.BlockSpec((tm, d), lambda i, j: (i, 0)),
        out_shape=jax.ShapeDtypeStruct((t, d), F32),
        scratch_shapes=[pltpu.VMEM((tm, d), BF16), pltpu.VMEM((tm, d), F32)],
        compiler_params=pltpu.CompilerParams(
            dimension_semantics=("arbitrary", "arbitrary"), vmem_limit_bytes=VMEM_LIMIT_BYTES),
        name="ffn",
    )(x, g_ffn, w1, w2, g_final)


def kernel(x, mem, g_mix, w_in, ln_v_g, ln_v_b, w_s, b_s, conv_w, g_mem, w_kv, g_head, w_o,
           g_ffn, w_ffn1, w_ffn2, g_final):
    bsz, seq, d = x.shape
    assert w_in.shape[0] == 1, "the FFN call fuses the final norm, so exactly one layer is supported"
    kv, w_in_b, w_o_b = _kv_call(mem, g_mem[0][None, :], w_kv[0], w_in[0], w_o[0])
    bias_full = jnp.repeat(b_s[0].T, HEAD_DIM, axis=1)
    x1, w1b, w2b = _mixer_call(x, kv, g_mix[0][None, :], w_in_b, ln_v_g[0][None, :],
                               ln_v_b[0][None, :], w_s[0], bias_full, conv_w[0], g_head[0][None, :],
                               w_o_b, w_ffn1[0], w_ffn2[0])
    y = _ffn_call(x1.reshape(bsz * seq, d), g_ffn[0][None, :], w1b, w2b, g_final[None, :])
    return y.reshape(bsz, seq, d)
```

```python
import functools

import jax
import jax.numpy as jnp
from jax import lax
from jax.experimental import pallas as pl
from jax.experimental.pallas import tpu as pltpu

HEAD_DIM = 128
N_MEM_HEADS = 4
CHUNK = 128
CONV_W = 3
EPS = 1e-6

SEQ_TILE = 256
FFN_ROW_TILE = 512
FFN_COL_TILE = 2048
FFN_SUB_TILES = 4
VMEM_LIMIT_BYTES = 58 * 1024 * 1024

F32 = jnp.float32
BF16 = jnp.bfloat16


def _rms(xf, g):
    return xf * lax.rsqrt(jnp.mean(xf * xf, axis=-1, keepdims=True) + EPS) * g


_dot = functools.partial(jnp.dot, preferred_element_type=F32)


def _kv_kernel(mem_ref, g_ref, w_ref, w_in_ref, w_o_ref, kv_ref, w_in_b_ref, w_o_b_ref):
    m = _rms(mem_ref[...], g_ref[...]).astype(BF16)
    kv_ref[...] = _dot(m, w_ref[...].astype(BF16)).astype(BF16)
    w_in_b_ref[...] = w_in_ref[...].astype(BF16)
    w_o_b_ref[...] = w_o_ref[...].astype(BF16)


def _kv_call(mem, g_mem, w_kv, w_in, w_o):
    bsz, n_mem, d = mem.shape
    n_out = w_kv.shape[1]
    assert w_in.shape[0] % (16 * bsz) == 0 and w_o.shape[0] % (16 * bsz) == 0
    r_in, r_o = w_in.shape[0] // bsz, w_o.shape[0] // bsz
    slab = lambda b: (b, 0)
    return pl.pallas_call(
        _kv_kernel,
        grid=(bsz,),
        in_specs=[
            pl.BlockSpec((None, n_mem, d), lambda b: (b, 0, 0)),
            pl.BlockSpec((1, d), lambda b: (0, 0)),
            pl.BlockSpec((d, n_out), lambda b: (0, 0), pipeline_mode=pl.Buffered(1)),
            pl.BlockSpec((r_in, w_in.shape[1]), slab),
            pl.BlockSpec((r_o, w_o.shape[1]), slab),
        ],
        out_specs=[pl.BlockSpec((None, n_mem, n_out), lambda b: (b, 0, 0)),
                   pl.BlockSpec((r_in, w_in.shape[1]), slab),
                   pl.BlockSpec((r_o, w_o.shape[1]), slab)],
        out_shape=[jax.ShapeDtypeStruct((bsz, n_mem, n_out), BF16),
                   jax.ShapeDtypeStruct(w_in.shape, BF16),
                   jax.ShapeDtypeStruct(w_o.shape, BF16)],
        compiler_params=pltpu.CompilerParams(
            dimension_semantics=("arbitrary",), vmem_limit_bytes=VMEM_LIMIT_BYTES),
        name="mem_kv",
    )(mem, g_mem, w_kv, w_in, w_o)


def _mixer_kernel(x_ref, kv_ref, g_mix_ref, w_in_ref, ln_g_ref, ln_b_ref, w_s_ref,
                  bias_ref, conv_w_ref, g_head_ref, w_o_ref, w1_ref, w2_ref,
                  o_ref, w1b_ref, w2b_ref,
                  heads_a_ref, heads_b_ref, heads_c_ref, carry_ref, *, ts, d_sgu, d_conv, d_mem):
    s = pl.program_id(1)
    n_chunks = ts // CHUNK
    n_sgu_heads = d_sgu // HEAD_DIM

    @pl.when(s == 0)
    def _():
        carry_ref[...] = jnp.zeros_like(carry_ref)

    x = x_ref[...]
    h = _rms(x, g_mix_ref[...]).astype(BF16)
    o_conv = 2 * d_sgu
    o_q = o_conv + 3 * d_conv

    def head_norm(a, col):
        y = a * lax.rsqrt(jnp.mean(a * a, axis=-1, keepdims=True) + EPS)
        return (y * g_head_ref[:, col:col + HEAD_DIM]).astype(BF16)

    uv = _dot(h, w_in_ref[:, 0:2 * d_sgu])
    q = _dot(h, w_in_ref[:, o_q:o_q + d_mem]) * (HEAD_DIM ** -0.5)
    cx = _dot(h, w_in_ref[:, o_conv + d_conv:o_conv + 3 * d_conv])

    u = jax.nn.gelu(uv[:, 0:d_sgu])
    v = jax.nn.gelu(uv[:, d_sgu:2 * d_sgu])
    mu = jnp.mean(v, axis=-1, keepdims=True)
    vc = v - mu
    vn = vc * lax.rsqrt(jnp.mean(vc * vc, axis=-1, keepdims=True) + EPS)
    vn = (vn * ln_g_ref[...] + ln_b_ref[...]).astype(BF16)

    scores = []
    for hd in range(N_MEM_HEADS):
        col = hd * HEAD_DIM
        q_h = q[:, col:col + HEAD_DIM].astype(BF16)
        k_h = kv_ref[:, col:col + HEAD_DIM]
        scores.append(lax.dot_general(q_h, k_h, (((1,), (1,)), ((), ())),
                                      preferred_element_type=F32))
    b_gate = _dot(h, w_in_ref[:, o_conv:o_conv + d_conv])

    w1b_ref[...] = w1_ref[...].astype(BF16)
    w2b_ref[...] = w2_ref[...].astype(BF16)

    t_idx = lax.broadcasted_iota(jnp.int32, (CHUNK, CHUNK), 0)
    s_idx = lax.broadcasted_iota(jnp.int32, (CHUNK, CHUNK), 1)
    causal = s_idx <= t_idx
    mixed = []
    for hd in range(n_sgu_heads):
        col = hd * HEAD_DIM
        w = jnp.where(causal, w_s_ref[hd], 0.0).astype(BF16)
        v_h = jnp.concatenate(
            [vn[c * CHUNK:(c + 1) * CHUNK, col:col + HEAD_DIM] for c in range(n_chunks)], axis=1)
        mixed.append(_dot(w, v_h))

    pv = []
    for hd in range(N_MEM_HEADS):
        col = hd * HEAD_DIM
        sc = scores[hd]
        e = jnp.exp(sc - jnp.max(sc, axis=-1, keepdims=True))
        denom = jnp.sum(e, axis=-1, keepdims=True)
        v_h = kv_ref[:, d_mem + col:d_mem + col + HEAD_DIM]
        pv.append((_dot(e.astype(BF16), v_h), denom))

    xc = cx[:, 0:d_conv] * cx[:, d_conv:2 * d_conv]
    prev = carry_ref[...]
    rows = lax.broadcasted_iota(jnp.int32, (ts, d_conv), 0)
    x_m1 = jnp.where(rows == 0, prev[7:8, :], pltpu.roll(xc, 1, 0))
    x_m2 = jnp.where(rows == 0, prev[6:7, :],
                     jnp.where(rows == 1, prev[7:8, :], pltpu.roll(xc, 2, 0)))
    y = (conv_w_ref[0:1, :] * x_m2 + conv_w_ref[1:2, :] * x_m1 + conv_w_ref[2:3, :] * xc)
    c_out = b_gate * y
    carry_ref[...] = xc[ts - 8:ts, :]
    for hd in range(d_conv // HEAD_DIM):
        col = hd * HEAD_DIM
        heads_b_ref[:, col:col + HEAD_DIM] = head_norm(c_out[:, col:col + HEAD_DIM], d_sgu + col)
    out = x + _dot(heads_b_ref[...], w_o_ref[d_sgu:d_sgu + d_conv, :])

    for hd in range(n_sgu_heads):
        col = hd * HEAD_DIM
        bias = bias_ref[:, col:col + HEAD_DIM]
        for c in range(n_chunks):
            r0 = c * CHUNK
            a = u[r0:r0 + CHUNK, col:col + HEAD_DIM] * (
                mixed[hd][:, c * HEAD_DIM:(c + 1) * HEAD_DIM] + bias)
            heads_a_ref[r0:r0 + CHUNK, col:col + HEAD_DIM] = head_norm(a, col)
    out = out + _dot(heads_a_ref[...], w_o_ref[0:d_sgu, :])

    for hd in range(N_MEM_HEADS):
        col = hd * HEAD_DIM
        o_h, denom = pv[hd]
        heads_c_ref[:, col:col + HEAD_DIM] = head_norm(o_h / denom, d_sgu + d_conv + col)
    out = out + _dot(heads_c_ref[...], w_o_ref[d_sgu + d_conv:d_sgu + d_conv + d_mem, :])
    o_ref[...] = out


def _mixer_call(x, kv, g_mix, w_in, ln_g, ln_b, w_s, bias_full, conv_w, g_head, w_o, w1, w2):
    bsz, seq, d = x.shape
    d_in = w_in.shape[1]
    d_mix = w_o.shape[0]
    d_sgu = ln_g.shape[1]
    d_conv = conv_w.shape[1]
    d_mem = kv.shape[2] // 2
    n_mem = kv.shape[1]
    ts = SEQ_TILE
    assert seq % ts == 0 and ts % CHUNK == 0
    assert d_in == 2 * d_sgu + 3 * d_conv + d_mem and d_mix == d_sgu + d_conv + d_mem
    n_s = seq // ts
    n_steps = bsz * n_s
    assert w1.shape[0] % (16 * n_steps) == 0 and w2.shape[0] % (16 * n_steps) == 0
    r1, r2 = w1.shape[0] // n_steps, w2.shape[0] // n_steps
    slab = lambda b, s: (b * n_s + s, 0)
    const = lambda b, s: (0, 0)
    resident = pl.Buffered(1)
    kern = functools.partial(_mixer_kernel, ts=ts, d_sgu=d_sgu, d_conv=d_conv, d_mem=d_mem)
    return pl.pallas_call(
        kern,
        grid=(bsz, seq // ts),
        in_specs=[
            pl.BlockSpec((None, ts, d), lambda b, s: (b, s, 0)),
            pl.BlockSpec((None, n_mem, 2 * d_mem), lambda b, s: (b, 0, 0)),
            pl.BlockSpec((1, d), const),
            pl.BlockSpec((d, d_in), const, pipeline_mode=resident),
            pl.BlockSpec((1, d_sgu), const),
            pl.BlockSpec((1, d_sgu), const),
            pl.BlockSpec(w_s.shape, lambda b, s: (0, 0, 0)),
            pl.BlockSpec((CHUNK, d_sgu), const),
            pl.BlockSpec((CONV_W, d_conv), const),
            pl.BlockSpec((1, d_mix), const),
            pl.BlockSpec((d_mix, d), const, pipeline_mode=resident),
            pl.BlockSpec((r1, w1.shape[1]), slab),
            pl.BlockSpec((r2, w2.shape[1]), slab),
        ],
        out_specs=[pl.BlockSpec((None, ts, d), lambda b, s: (b, s, 0)),
                   pl.BlockSpec((r1, w1.shape[1]), slab),
                   pl.BlockSpec((r2, w2.shape[1]), slab)],
        out_shape=[jax.ShapeDtypeStruct((bsz, seq, d), F32),
                   jax.ShapeDtypeStruct(w1.shape, BF16),
                   jax.ShapeDtypeStruct(w2.shape, BF16)],
        scratch_shapes=[
            pltpu.VMEM((ts, d_sgu), BF16),
            pltpu.VMEM((ts, d_conv), BF16),
            pltpu.VMEM((ts, d_mem), BF16),
            pltpu.VMEM((8, d_conv), F32),
        ],
        compiler_params=pltpu.CompilerParams(
            dimension_semantics=("arbitrary", "arbitrary"), vmem_limit_bytes=VMEM_LIMIT_BYTES),
        name="mixer",
    )(x, kv, g_mix, w_in, ln_g, ln_b, w_s, bias_full, conv_w, g_head, w_o, w1, w2)


def _ffn_kernel(x_ref, g_ffn_ref, w1_ref, w2_ref, g_final_ref, o_ref, hn_ref):
    j = pl.program_id(1)
    sub = w1_ref.shape[1] // FFN_SUB_TILES

    @pl.when(j == 0)
    def _():
        hn_ref[...] = _rms(x_ref[...], g_ffn_ref[...]).astype(BF16)
        o_ref[...] = jnp.zeros_like(o_ref)

    hn = hn_ref[...]
    for c in range(FFN_SUB_TILES):
        f = jnp.maximum(_dot(hn, w1_ref[:, c * sub:(c + 1) * sub]), 0.0)
        o_ref[...] += _dot((f * f).astype(BF16), w2_ref[c * sub:(c + 1) * sub, :])

    @pl.when(j == pl.num_programs(1) - 1)
    def _():
        o_ref[...] = _rms(x_ref[...] + o_ref[...], g_final_ref[...])


def _ffn_call(x, g_ffn, w1, w2, g_final):
    t, d = x.shape
    d_ff = w1.shape[1]
    tm, tf = FFN_ROW_TILE, FFN_COL_TILE
    assert t % tm == 0 and d_ff % tf == 0 and tf % (FFN_SUB_TILES * 128) == 0
    return pl.pallas_call(
        _ffn_kernel,
        grid=(t // tm, d_ff // tf),
        in_specs=[
            pl.BlockSpec((tm, d), lambda i, j: (i, 0)),
            pl.BlockSpec((1, d), lambda i, j: (0, 0)),
            pl.BlockSpec((d, tf), lambda i, j: (0, j)),
            pl.BlockSpec((tf, d), lambda i, j: (j, 0)),
            pl.BlockSpec((1, d), lambda i, j: (0, 0)),
        ],
        out_specs=pl.BlockSpec((tm, d), lambda i, j: (i, 0)),
        out_shape=jax.ShapeDtypeStruct((t, d), F32),
        scratch_shapes=[pltpu.VMEM((tm, d), BF16)],
        compiler_params=pltpu.CompilerParams(
            dimension_semantics=("arbitrary", "arbitrary"), vmem_limit_bytes=VMEM_LIMIT_BYTES),
        name="ffn",
    )(x, g_ffn, w1, w2, g_final)


def kernel(x, mem, g_mix, w_in, ln_v_g, ln_v_b, w_s, b_s, conv_w, g_mem, w_kv, g_head, w_o,
           g_ffn, w_ffn1, w_ffn2, g_final):
    bsz, seq, d = x.shape
    assert w_in.shape[0] == 1, "the FFN call fuses the final norm, so exactly one layer is supported"
    kv, w_in_b, w_o_b = _kv_call(mem, g_mem[0][None, :], w_kv[0], w_in[0], w_o[0])
    bias_full = jnp.repeat(b_s[0].T, HEAD_DIM, axis=1)
    x1, w1b, w2b = _mixer_call(x, kv, g_mix[0][None, :], w_in_b, ln_v_g[0][None, :],
                               ln_v_b[0][None, :], w_s[0], bias_full, conv_w[0], g_head[0][None, :],
                               w_o_b, w_ffn1[0], w_ffn2[0])
    y = _ffn_call(x1.reshape(bsz * seq, d), g_ffn[0][None, :], w1b, w2b, g_final[None, :])
    return y.reshape(bsz, seq, d)
```

```python
import functools

import jax
import jax.numpy as jnp
from jax import lax
from jax.experimental import pallas as pl
from jax.experimental.pallas import tpu as pltpu

HEAD_DIM = 128
N_MEM_HEADS = 4
CHUNK = 128
CONV_W = 3
EPS = 1e-6

SEQ_TILE = 256
FFN_ROW_TILE = 512
FFN_COL_TILE = 2048
FFN_SUB_TILES = 4
FFN_EDGE_GROUPS = 2
VMEM_LIMIT_BYTES = 58 * 1024 * 1024

F32 = jnp.float32
BF16 = jnp.bfloat16


def _rms(xf, g):
    return xf * lax.rsqrt(jnp.mean(xf * xf, axis=-1, keepdims=True) + EPS) * g


_dot = functools.partial(jnp.dot, preferred_element_type=F32)


def _kv_kernel(mem_ref, g_ref, w_ref, w_in_ref, w_o_ref, kv_ref, w_in_b_ref, w_o_b_ref):
    m = _rms(mem_ref[...], g_ref[...]).astype(BF16)
    kv_ref[...] = _dot(m, w_ref[...].astype(BF16)).astype(BF16)
    w_in_b_ref[...] = w_in_ref[...].astype(BF16)
    w_o_b_ref[...] = w_o_ref[...].astype(BF16)


def _kv_call(mem, g_mem, w_kv, w_in, w_o):
    bsz, n_mem, d = mem.shape
    n_out = w_kv.shape[1]
    assert w_in.shape[0] % (16 * bsz) == 0 and w_o.shape[0] % (16 * bsz) == 0
    r_in, r_o = w_in.shape[0] // bsz, w_o.shape[0] // bsz
    slab = lambda b: (b, 0)
    return pl.pallas_call(
        _kv_kernel,
        grid=(bsz,),
        in_specs=[
            pl.BlockSpec((None, n_mem, d), lambda b: (b, 0, 0)),
            pl.BlockSpec((1, d), lambda b: (0, 0)),
            pl.BlockSpec((d, n_out), lambda b: (0, 0), pipeline_mode=pl.Buffered(1)),
            pl.BlockSpec((r_in, w_in.shape[1]), slab),
            pl.BlockSpec((r_o, w_o.shape[1]), slab),
        ],
        out_specs=[pl.BlockSpec((None, n_mem, n_out), lambda b: (b, 0, 0)),
                   pl.BlockSpec((r_in, w_in.shape[1]), slab),
                   pl.BlockSpec((r_o, w_o.shape[1]), slab)],
        out_shape=[jax.ShapeDtypeStruct((bsz, n_mem, n_out), BF16),
                   jax.ShapeDtypeStruct(w_in.shape, BF16),
                   jax.ShapeDtypeStruct(w_o.shape, BF16)],
        compiler_params=pltpu.CompilerParams(
            dimension_semantics=("arbitrary",), vmem_limit_bytes=VMEM_LIMIT_BYTES),
        name="mem_kv",
    )(mem, g_mem, w_kv, w_in, w_o)


def _mixer_kernel(x_ref, kv_ref, g_mix_ref, w_in_ref, ln_g_ref, ln_b_ref, w_s_ref,
                  bias_ref, conv_w_ref, g_head_ref, w_o_ref, w1_ref, w2_ref,
                  o_ref, w1b_ref, w2b_ref,
                  heads_a_ref, heads_b_ref, heads_c_ref, carry_ref, *, ts, d_sgu, d_conv, d_mem):
    s = pl.program_id(1)
    n_chunks = ts // CHUNK
    n_sgu_heads = d_sgu // HEAD_DIM

    @pl.when(s == 0)
    def _():
        carry_ref[...] = jnp.zeros_like(carry_ref)

    x = x_ref[...]
    h = _rms(x, g_mix_ref[...]).astype(BF16)
    o_conv = 2 * d_sgu
    o_q = o_conv + 3 * d_conv

    def head_norm(a, col):
        y = a * lax.rsqrt(jnp.mean(a * a, axis=-1, keepdims=True) + EPS)
        return (y * g_head_ref[:, col:col + HEAD_DIM]).astype(BF16)

    uv = _dot(h, w_in_ref[:, 0:2 * d_sgu])
    q = _dot(h, w_in_ref[:, o_q:o_q + d_mem]) * (HEAD_DIM ** -0.5)
    cx = _dot(h, w_in_ref[:, o_conv + d_conv:o_conv + 3 * d_conv])

    u = jax.nn.gelu(uv[:, 0:d_sgu])
    v = jax.nn.gelu(uv[:, d_sgu:2 * d_sgu])
    mu = jnp.mean(v, axis=-1, keepdims=True)
    vc = v - mu
    vn = vc * lax.rsqrt(jnp.mean(vc * vc, axis=-1, keepdims=True) + EPS)
    vn = (vn * ln_g_ref[...] + ln_b_ref[...]).astype(BF16)

    scores = []
    for hd in range(N_MEM_HEADS):
        col = hd * HEAD_DIM
        q_h = q[:, col:col + HEAD_DIM].astype(BF16)
        k_h = kv_ref[:, col:col + HEAD_DIM]
        scores.append(lax.dot_general(q_h, k_h, (((1,), (1,)), ((), ())),
                                      preferred_element_type=F32))
    b_gate = _dot(h, w_in_ref[:, o_conv:o_conv + d_conv])

    w1b_ref[...] = w1_ref[...].astype(BF16)
    w2b_ref[...] = w2_ref[...].astype(BF16)

    t_idx = lax.broadcasted_iota(jnp.int32, (CHUNK, CHUNK), 0)
    s_idx = lax.broadcasted_iota(jnp.int32, (CHUNK, CHUNK), 1)
    causal = s_idx <= t_idx
    mixed = []
    for hd in range(n_sgu_heads):
        col = hd * HEAD_DIM
        w = jnp.where(causal, w_s_ref[hd], 0.0).astype(BF16)
        v_h = jnp.concatenate(
            [vn[c * CHUNK:(c + 1) * CHUNK, col:col + HEAD_DIM] for c in range(n_chunks)], axis=1)
        mixed.append(_dot(w, v_h))

    pv = []
    for hd in range(N_MEM_HEADS):
        col = hd * HEAD_DIM
        sc = scores[hd]
        e = jnp.exp(sc - jnp.max(sc, axis=-1, keepdims=True))
        denom = jnp.sum(e, axis=-1, keepdims=True)
        v_h = kv_ref[:, d_mem + col:d_mem + col + HEAD_DIM]
        pv.append((_dot(e.astype(BF16), v_h), denom))

    xc = cx[:, 0:d_conv] * cx[:, d_conv:2 * d_conv]
    prev = carry_ref[...]
    rows = lax.broadcasted_iota(jnp.int32, (ts, d_conv), 0)
    x_m1 = jnp.where(rows == 0, prev[7:8, :], pltpu.roll(xc, 1, 0))
    x_m2 = jnp.where(rows == 0, prev[6:7, :],
                     jnp.where(rows == 1, prev[7:8, :], pltpu.roll(xc, 2, 0)))
    y = (conv_w_ref[0:1, :] * x_m2 + conv_w_ref[1:2, :] * x_m1 + conv_w_ref[2:3, :] * xc)
    c_out = b_gate * y
    carry_ref[...] = xc[ts - 8:ts, :]
    for hd in range(d_conv // HEAD_DIM):
        col = hd * HEAD_DIM
        heads_b_ref[:, col:col + HEAD_DIM] = head_norm(c_out[:, col:col + HEAD_DIM], d_sgu + col)
    out = x + _dot(heads_b_ref[...], w_o_ref[d_sgu:d_sgu + d_conv, :])

    for hd in range(n_sgu_heads):
        col = hd * HEAD_DIM
        bias = bias_ref[:, col:col + HEAD_DIM]
        for c in range(n_chunks):
            r0 = c * CHUNK
            a = u[r0:r0 + CHUNK, col:col + HEAD_DIM] * (
                mixed[hd][:, c * HEAD_DIM:(c + 1) * HEAD_DIM] + bias)
            heads_a_ref[r0:r0 + CHUNK, col:col + HEAD_DIM] = head_norm(a, col)
    out = out + _dot(heads_a_ref[...], w_o_ref[0:d_sgu, :])

    for hd in range(N_MEM_HEADS):
        col = hd * HEAD_DIM
        o_h, denom = pv[hd]
        heads_c_ref[:, col:col + HEAD_DIM] = head_norm(o_h / denom, d_sgu + d_conv + col)
    out = out + _dot(heads_c_ref[...], w_o_ref[d_sgu + d_conv:d_sgu + d_conv + d_mem, :])
    o_ref[...] = out


def _mixer_call(x, kv, g_mix, w_in, ln_g, ln_b, w_s, bias_full, conv_w, g_head, w_o, w1, w2):
    bsz, seq, d = x.shape
    d_in = w_in.shape[1]
    d_mix = w_o.shape[0]
    d_sgu = ln_g.shape[1]
    d_conv = conv_w.shape[1]
    d_mem = kv.shape[2] // 2
    n_mem = kv.shape[1]
    ts = SEQ_TILE
    assert seq % ts == 0 and ts % CHUNK == 0
    assert d_in == 2 * d_sgu + 3 * d_conv + d_mem and d_mix == d_sgu + d_conv + d_mem
    n_s = seq // ts
    n_steps = bsz * n_s
    assert w1.shape[0] % (16 * n_steps) == 0 and w2.shape[0] % (16 * n_steps) == 0
    r1, r2 = w1.shape[0] // n_steps, w2.shape[0] // n_steps
    slab = lambda b, s: (b * n_s + s, 0)
    const = lambda b, s: (0, 0)
    resident = pl.Buffered(1)
    kern = functools.partial(_mixer_kernel, ts=ts, d_sgu=d_sgu, d_conv=d_conv, d_mem=d_mem)
    return pl.pallas_call(
        kern,
        grid=(bsz, seq // ts),
        in_specs=[
            pl.BlockSpec((None, ts, d), lambda b, s: (b, s, 0)),
            pl.BlockSpec((None, n_mem, 2 * d_mem), lambda b, s: (b, 0, 0)),
            pl.BlockSpec((1, d), const),
            pl.BlockSpec((d, d_in), const, pipeline_mode=resident),
            pl.BlockSpec((1, d_sgu), const),
            pl.BlockSpec((1, d_sgu), const),
            pl.BlockSpec(w_s.shape, lambda b, s: (0, 0, 0)),
            pl.BlockSpec((CHUNK, d_sgu), const),
            pl.BlockSpec((CONV_W, d_conv), const),
            pl.BlockSpec((1, d_mix), const),
            pl.BlockSpec((d_mix, d), const, pipeline_mode=resident),
            pl.BlockSpec((r1, w1.shape[1]), slab),
            pl.BlockSpec((r2, w2.shape[1]), slab),
        ],
        out_specs=[pl.BlockSpec((None, ts, d), lambda b, s: (b, s, 0)),
                   pl.BlockSpec((r1, w1.shape[1]), slab),
                   pl.BlockSpec((r2, w2.shape[1]), slab)],
        out_shape=[jax.ShapeDtypeStruct((bsz, seq, d), F32),
                   jax.ShapeDtypeStruct(w1.shape, BF16),
                   jax.ShapeDtypeStruct(w2.shape, BF16)],
        scratch_shapes=[
            pltpu.VMEM((ts, d_sgu), BF16),
            pltpu.VMEM((ts, d_conv), BF16),
            pltpu.VMEM((ts, d_mem), BF16),
            pltpu.VMEM((8, d_conv), F32),
        ],
        compiler_params=pltpu.CompilerParams(
            dimension_semantics=("arbitrary", "arbitrary"), vmem_limit_bytes=VMEM_LIMIT_BYTES),
        name="mixer",
    )(x, kv, g_mix, w_in, ln_g, ln_b, w_s, bias_full, conv_w, g_head, w_o, w1, w2)


def _ffn_kernel(x_ref, g_ffn_ref, w1_ref, w2_ref, g_final_ref, o_ref, hn_ref):
    j = pl.program_id(1)
    last = pl.num_programs(1) - 1
    tm = x_ref.shape[0]
    sub = w1_ref.shape[1] // FFN_SUB_TILES

    def matmuls(rows, first):
        hn = hn_ref[rows, :]
        for c in range(FFN_SUB_TILES):
            f = jnp.maximum(_dot(hn, w1_ref[:, c * sub:(c + 1) * sub]), 0.0)
            y = _dot((f * f).astype(BF16), w2_ref[c * sub:(c + 1) * sub, :])
            if first and c == 0:
                o_ref[rows, :] = y
            else:
                o_ref[rows, :] += y

    groups = [pl.ds(g * (tm // FFN_EDGE_GROUPS), tm // FFN_EDGE_GROUPS) for g in range(FFN_EDGE_GROUPS)]

    @pl.when(j == 0)
    def _():
        for rows in groups:
            hn_ref[rows, :] = _rms(x_ref[rows, :], g_ffn_ref[...]).astype(BF16)
        for rows in groups:
            matmuls(rows, first=True)

    @pl.when(jnp.logical_and(j > 0, j < last))
    def _():
        matmuls(pl.ds(0, tm), first=False)

    @pl.when(j == last)
    def _():
        for rows in groups:
            matmuls(rows, first=False)
            o_ref[rows, :] = _rms(x_ref[rows, :] + o_ref[rows, :], g_final_ref[...])


def _ffn_call(x, g_ffn, w1, w2, g_final):
    t, d = x.shape
    d_ff = w1.shape[1]
    tm, tf = FFN_ROW_TILE, FFN_COL_TILE
    assert t % tm == 0 and d_ff % tf == 0 and tf % (FFN_SUB_TILES * 128) == 0
    assert d_ff // tf >= 2 and tm % (FFN_EDGE_GROUPS * 16) == 0
    return pl.pallas_call(
        _ffn_kernel,
        grid=(t // tm, d_ff // tf),
        in_specs=[
            pl.BlockSpec((tm, d), lambda i, j: (i, 0)),
            pl.BlockSpec((1, d), lambda i, j: (0, 0)),
            pl.BlockSpec((d, tf), lambda i, j: (0, j)),
            pl.BlockSpec((tf, d), lambda i, j: (j, 0)),
            pl.BlockSpec((1, d), lambda i, j: (0, 0)),
        ],
        out_specs=pl.BlockSpec((tm, d), lambda i, j: (i, 0)),
        out_shape=jax.ShapeDtypeStruct((t, d), F32),
        scratch_shapes=[pltpu.VMEM((tm, d), BF16)],
        compiler_params=pltpu.CompilerParams(
            dimension_semantics=("arbitrary", "arbitrary"), vmem_limit_bytes=VMEM_LIMIT_BYTES),
        name="ffn",
    )(x, g_ffn, w1, w2, g_final)


def kernel(x, mem, g_mix, w_in, ln_v_g, ln_v_b, w_s, b_s, conv_w, g_mem, w_kv, g_head, w_o,
           g_ffn, w_ffn1, w_ffn2, g_final):
    bsz, seq, d = x.shape
    assert w_in.shape[0] == 1, "the FFN call fuses the final norm, so exactly one layer is supported"
    kv, w_in_b, w_o_b = _kv_call(mem, g_mem[0][None, :], w_kv[0], w_in[0], w_o[0])
    bias_full = jnp.repeat(b_s[0].T, HEAD_DIM, axis=1)
    x1, w1b, w2b = _mixer_call(x, kv, g_mix[0][None, :], w_in_b, ln_v_g[0][None, :],
                               ln_v_b[0][None, :], w_s[0], bias_full, conv_w[0], g_head[0][None, :],
                               w_o_b, w_ffn1[0], w_ffn2[0])
    y = _ffn_call(x1.reshape(bsz * seq, d), g_ffn[0][None, :], w1b, w2b, g_final[None, :])
    return y.reshape(bsz, seq, d)
```

```python
import functools

import jax
import jax.numpy as jnp
from jax import lax
from jax.experimental import pallas as pl
from jax.experimental.pallas import tpu as pltpu

HEAD_DIM = 128
N_MEM_HEADS = 4
CHUNK = 128
CONV_W = 3
EPS = 1e-6

SEQ_TILE = 256
FFN_ROW_TILE = 512
FFN_COL_TILE = 2048
FFN_SUB_TILES = 4
WEIGHT_CHUNKS = 32
WEIGHT_STAGE_SLOTS = 4
VMEM_LIMIT_BYTES = 58 * 1024 * 1024

F32 = jnp.float32
BF16 = jnp.bfloat16


def _rms(xf, g):
    return xf * lax.rsqrt(jnp.mean(xf * xf, axis=-1, keepdims=True) + EPS) * g


_dot = functools.partial(jnp.dot, preferred_element_type=F32)


def _kv_kernel(mem_ref, g_ref, w_ref, kv_ref):
    m = _rms(mem_ref[...], g_ref[...]).astype(BF16)
    kv_ref[...] = _dot(m, w_ref[...].astype(BF16)).astype(BF16)


def _kv_call(mem, g_mem, w_kv):
    bsz, n_mem, d = mem.shape
    n_out = w_kv.shape[1]
    return pl.pallas_call(
        _kv_kernel,
        grid=(bsz,),
        in_specs=[
            pl.BlockSpec((None, n_mem, d), lambda b: (b, 0, 0)),
            pl.BlockSpec((1, d), lambda b: (0, 0)),
            pl.BlockSpec((d, n_out), lambda b: (0, 0), pipeline_mode=pl.Buffered(1)),
        ],
        out_specs=pl.BlockSpec((None, n_mem, n_out), lambda b: (b, 0, 0)),
        out_shape=jax.ShapeDtypeStruct((bsz, n_mem, n_out), BF16),
        compiler_params=pltpu.CompilerParams(
            dimension_semantics=("arbitrary",), vmem_limit_bytes=VMEM_LIMIT_BYTES),
        name="mem_kv",
    )(mem, g_mem, w_kv)


def _load_rounded(srcs, dsts, stages, sems):
    n_slots = stages[0].shape[0]
    n = srcs[0].shape[0] // stages[0].shape[1]
    assert all(s.shape[0] == n * st.shape[1] and st.shape[0] == n_slots for s, st in zip(srcs, stages))

    def rows_of(i, k):
        r = stages[i].shape[1]
        return pl.ds(k * r, r)

    def copy(i, k):
        return pltpu.make_async_copy(srcs[i].at[rows_of(i, k), :], stages[i].at[k % n_slots],
                                     sems.at[i, k % n_slots])

    for k in range(n_slots - 1):
        for i in range(len(srcs)):
            copy(i, k).start()
    for k in range(n):
        for i in range(len(srcs)):
            if k + n_slots - 1 < n:
                copy(i, k + n_slots - 1).start()
            copy(i, k).wait()
            dsts[i][rows_of(i, k), :] = stages[i][k % n_slots].astype(BF16)


def _mixer_kernel(x_ref, kv_ref, g_mix_ref, w_in_hbm, ln_g_ref, ln_b_ref, w_s_ref,
                  bias_ref, conv_w_ref, g_head_ref, w_o_hbm, w1_ref, w2_ref,
                  o_ref, w1b_ref, w2b_ref,
                  w_in_ref, w_o_ref, stage_in, stage_o, sems,
                  heads_a_ref, heads_b_ref, heads_c_ref, carry_ref, *, ts, d_sgu, d_conv, d_mem):
    s = pl.program_id(1)

    @pl.when(jnp.logical_and(pl.program_id(0) == 0, s == 0))
    def _():
        _load_rounded([w_in_hbm, w_o_hbm], [w_in_ref, w_o_ref], [stage_in, stage_o], sems)

    n_chunks = ts // CHUNK
    n_sgu_heads = d_sgu // HEAD_DIM

    @pl.when(s == 0)
    def _():
        carry_ref[...] = jnp.zeros_like(carry_ref)

    x = x_ref[...]
    h = _rms(x, g_mix_ref[...]).astype(BF16)
    o_conv = 2 * d_sgu
    o_q = o_conv + 3 * d_conv

    def head_norm(a, col):
        y = a * lax.rsqrt(jnp.mean(a * a, axis=-1, keepdims=True) + EPS)
        return (y * g_head_ref[:, col:col + HEAD_DIM]).astype(BF16)

    uv = _dot(h, w_in_ref[:, 0:2 * d_sgu])
    q = _dot(h, w_in_ref[:, o_q:o_q + d_mem]) * (HEAD_DIM ** -0.5)
    cx = _dot(h, w_in_ref[:, o_conv + d_conv:o_conv + 3 * d_conv])

    u = jax.nn.gelu(uv[:, 0:d_sgu])
    v = jax.nn.gelu(uv[:, d_sgu:2 * d_sgu])
    mu = jnp.mean(v, axis=-1, keepdims=True)
    vc = v - mu
    vn = vc * lax.rsqrt(jnp.mean(vc * vc, axis=-1, keepdims=True) + EPS)
    vn = (vn * ln_g_ref[...] + ln_b_ref[...]).astype(BF16)

    scores = []
    for hd in range(N_MEM_HEADS):
        col = hd * HEAD_DIM
        q_h = q[:, col:col + HEAD_DIM].astype(BF16)
        k_h = kv_ref[:, col:col + HEAD_DIM]
        scores.append(lax.dot_general(q_h, k_h, (((1,), (1,)), ((), ())),
                                      preferred_element_type=F32))
    b_gate = _dot(h, w_in_ref[:, o_conv:o_conv + d_conv])

    w1b_ref[...] = w1_ref[...].astype(BF16)
    w2b_ref[...] = w2_ref[...].astype(BF16)

    t_idx = lax.broadcasted_iota(jnp.int32, (CHUNK, CHUNK), 0)
    s_idx = lax.broadcasted_iota(jnp.int32, (CHUNK, CHUNK), 1)
    causal = s_idx <= t_idx
    mixed = []
    for hd in range(n_sgu_heads):
        col = hd * HEAD_DIM
        w = jnp.where(causal, w_s_ref[hd], 0.0).astype(BF16)
        v_h = jnp.concatenate(
            [vn[c * CHUNK:(c + 1) * CHUNK, col:col + HEAD_DIM] for c in range(n_chunks)], axis=1)
        mixed.append(_dot(w, v_h))

    pv = []
    for hd in range(N_MEM_HEADS):
        col = hd * HEAD_DIM
        sc = scores[hd]
        e = jnp.exp(sc - jnp.max(sc, axis=-1, keepdims=True))
        denom = jnp.sum(e, axis=-1, keepdims=True)
        v_h = kv_ref[:, d_mem + col:d_mem + col + HEAD_DIM]
        pv.append((_dot(e.astype(BF16), v_h), denom))

    xc = cx[:, 0:d_conv] * cx[:, d_conv:2 * d_conv]
    prev = carry_ref[...]
    rows = lax.broadcasted_iota(jnp.int32, (ts, d_conv), 0)
    x_m1 = jnp.where(rows == 0, prev[7:8, :], pltpu.roll(xc, 1, 0))
    x_m2 = jnp.where(rows == 0, prev[6:7, :],
                     jnp.where(rows == 1, prev[7:8, :], pltpu.roll(xc, 2, 0)))
    y = (conv_w_ref[0:1, :] * x_m2 + conv_w_ref[1:2, :] * x_m1 + conv_w_ref[2:3, :] * xc)
    c_out = b_gate * y
    carry_ref[...] = xc[ts - 8:ts, :]
    for hd in range(d_conv // HEAD_DIM):
        col = hd * HEAD_DIM
        heads_b_ref[:, col:col + HEAD_DIM] = head_norm(c_out[:, col:col + HEAD_DIM], d_sgu + col)
    out = x + _dot(heads_b_ref[...], w_o_ref[d_sgu:d_sgu + d_conv, :])

    for hd in range(n_sgu_heads):
        col = hd * HEAD_DIM
        bias = bias_ref[:, col:col + HEAD_DIM]
        for c in range(n_chunks):
            r0 = c * CHUNK
            a = u[r0:r0 + CHUNK, col:col + HEAD_DIM] * (
                mixed[hd][:, c * HEAD_DIM:(c + 1) * HEAD_DIM] + bias)
            heads_a_ref[r0:r0 + CHUNK, col:col + HEAD_DIM] = head_norm(a, col)
    out = out + _dot(heads_a_ref[...], w_o_ref[0:d_sgu, :])

    for hd in range(N_MEM_HEADS):
        col = hd * HEAD_DIM
        o_h, denom = pv[hd]
        heads_c_ref[:, col:col + HEAD_DIM] = head_norm(o_h / denom, d_sgu + d_conv + col)
    out = out + _dot(heads_c_ref[...], w_o_ref[d_sgu + d_conv:d_sgu + d_conv + d_mem, :])
    o_ref[...] = out


def _mixer_call(x, kv, g_mix, w_in, ln_g, ln_b, w_s, bias_full, conv_w, g_head, w_o, w1, w2):
    bsz, seq, d = x.shape
    d_in = w_in.shape[1]
    d_mix = w_o.shape[0]
    d_sgu = ln_g.shape[1]
    d_conv = conv_w.shape[1]
    d_mem = kv.shape[2] // 2
    n_mem = kv.shape[1]
    ts = SEQ_TILE
    assert seq % ts == 0 and ts % CHUNK == 0
    assert d_in == 2 * d_sgu + 3 * d_conv + d_mem and d_mix == d_sgu + d_conv + d_mem
    assert d % (16 * WEIGHT_CHUNKS) == 0 and d_mix % (16 * WEIGHT_CHUNKS) == 0
    n_s = seq // ts
    n_steps = bsz * n_s
    assert w1.shape[0] % (16 * n_steps) == 0 and w2.shape[0] % (16 * n_steps) == 0
    r1, r2 = w1.shape[0] // n_steps, w2.shape[0] // n_steps
    slab = lambda b, s: (b * n_s + s, 0)
    const = lambda b, s: (0, 0)
    kern = functools.partial(_mixer_kernel, ts=ts, d_sgu=d_sgu, d_conv=d_conv, d_mem=d_mem)
    return pl.pallas_call(
        kern,
        grid=(bsz, seq // ts),
        in_specs=[
            pl.BlockSpec((None, ts, d), lambda b, s: (b, s, 0)),
            pl.BlockSpec((None, n_mem, 2 * d_mem), lambda b, s: (b, 0, 0)),
            pl.BlockSpec((1, d), const),
            pl.BlockSpec(memory_space=pl.ANY),
            pl.BlockSpec((1, d_sgu), const),
            pl.BlockSpec((1, d_sgu), const),
            pl.BlockSpec(w_s.shape, lambda b, s: (0, 0, 0)),
            pl.BlockSpec((CHUNK, d_sgu), const),
            pl.BlockSpec((CONV_W, d_conv), const),
            pl.BlockSpec((1, d_mix), const),
            pl.BlockSpec(memory_space=pl.ANY),
            pl.BlockSpec((r1, w1.shape[1]), slab),
            pl.BlockSpec((r2, w2.shape[1]), slab),
        ],
        out_specs=[pl.BlockSpec((None, ts, d), lambda b, s: (b, s, 0)),
                   pl.BlockSpec((r1, w1.shape[1]), slab),
                   pl.BlockSpec((r2, w2.shape[1]), slab)],
        out_shape=[jax.ShapeDtypeStruct((bsz, seq, d), F32),
                   jax.ShapeDtypeStruct(w1.shape, BF16),
                   jax.ShapeDtypeStruct(w2.shape, BF16)],
        scratch_shapes=[
            pltpu.VMEM((d, d_in), BF16),
            pltpu.VMEM((d_mix, d), BF16),
            pltpu.VMEM((WEIGHT_STAGE_SLOTS, d // WEIGHT_CHUNKS, d_in), F32),
            pltpu.VMEM((WEIGHT_STAGE_SLOTS, d_mix // WEIGHT_CHUNKS, d), F32),
            pltpu.SemaphoreType.DMA((2, WEIGHT_STAGE_SLOTS)),
            pltpu.VMEM((ts, d_sgu), BF16),
            pltpu.VMEM((ts, d_conv), BF16),
            pltpu.VMEM((ts, d_mem), BF16),
            pltpu.VMEM((8, d_conv), F32),
        ],
        compiler_params=pltpu.CompilerParams(
            dimension_semantics=("arbitrary", "arbitrary"), vmem_limit_bytes=VMEM_LIMIT_BYTES),
        name="mixer",
    )(x, kv, g_mix, w_in, ln_g, ln_b, w_s, bias_full, conv_w, g_head, w_o, w1, w2)


def _ffn_kernel(x_ref, g_ffn_ref, w1_ref, w2_ref, g_final_ref, o_ref, hn_ref):
    j = pl.program_id(1)
    sub = w1_ref.shape[1] // FFN_SUB_TILES

    @pl.when(j == 0)
    def _():
        hn_ref[...] = _rms(x_ref[...], g_ffn_ref[...]).astype(BF16)
        o_ref[...] = jnp.zeros_like(o_ref)

    hn = hn_ref[...]
    for c in range(FFN_SUB_TILES):
        f = jnp.maximum(_dot(hn, w1_ref[:, c * sub:(c + 1) * sub]), 0.0)
        o_ref[...] += _dot((f * f).astype(BF16), w2_ref[c * sub:(c + 1) * sub, :])

    @pl.when(j == pl.num_programs(1) - 1)
    def _():
        o_ref[...] = _rms(x_ref[...] + o_ref[...], g_final_ref[...])


def _ffn_call(x, g_ffn, w1, w2, g_final):
    t, d = x.shape
    d_ff = w1.shape[1]
    tm, tf = FFN_ROW_TILE, FFN_COL_TILE
    assert t % tm == 0 and d_ff % tf == 0 and tf % (FFN_SUB_TILES * 128) == 0
    return pl.pallas_call(
        _ffn_kernel,
        grid=(t // tm, d_ff // tf),
        in_specs=[
            pl.BlockSpec((tm, d), lambda i, j: (i, 0)),
            pl.BlockSpec((1, d), lambda i, j: (0, 0)),
            pl.BlockSpec((d, tf), lambda i, j: (0, j)),
            pl.BlockSpec((tf, d), lambda i, j: (j, 0)),
            pl.BlockSpec((1, d), lambda i, j: (0, 0)),
        ],
        out_specs=pl.BlockSpec((tm, d), lambda i, j: (i, 0)),
        out_shape=jax.ShapeDtypeStruct((t, d), F32),
        scratch_shapes=[pltpu.VMEM((tm, d), BF16)],
        compiler_params=pltpu.CompilerParams(
            dimension_semantics=("arbitrary", "arbitrary"), vmem_limit_bytes=VMEM_LIMIT_BYTES),
        name="ffn",
    )(x, g_ffn, w1, w2, g_final)


def kernel(x, mem, g_mix, w_in, ln_v_g, ln_v_b, w_s, b_s, conv_w, g_mem, w_kv, g_head, w_o,
           g_ffn, w_ffn1, w_ffn2, g_final):
    bsz, seq, d = x.shape
    assert w_in.shape[0] == 1, "the FFN call fuses the final norm, so exactly one layer is supported"
    kv = _kv_call(mem, g_mem[0][None, :], w_kv[0])
    bias_full = jnp.repeat(b_s[0].T, HEAD_DIM, axis=1)
    x1, w1b, w2b = _mixer_call(x, kv, g_mix[0][None, :], w_in[0], ln_v_g[0][None, :],
                               ln_v_b[0][None, :], w_s[0], bias_full, conv_w[0], g_head[0][None, :],
                               w_o[0], w_ffn1[0], w_ffn2[0])
    y = _ffn_call(x1.reshape(bsz * seq, d), g_ffn[0][None, :], w1b, w2b, g_final[None, :])
    return y.reshape(bsz, seq, d)
```

```python
import functools

import jax
import jax.numpy as jnp
from jax import lax
from jax.experimental import pallas as pl
from jax.experimental.pallas import tpu as pltpu

HEAD_DIM = 128
N_MEM_HEADS = 4
CHUNK = 128
CONV_W = 3
EPS = 1e-6

SEQ_TILE = 256
FFN_ROW_TILE = 512
FFN_COL_TILE = 2048
FFN_SUB_TILES = 4
WEIGHT_CHUNKS = 32
WEIGHT_STAGE_SLOTS = 4
VMEM_LIMIT_BYTES = 58 * 1024 * 1024

F32 = jnp.float32
BF16 = jnp.bfloat16


def _rms(xf, g):
    return xf * lax.rsqrt(jnp.mean(xf * xf, axis=-1, keepdims=True) + EPS) * g


_dot = functools.partial(jnp.dot, preferred_element_type=F32)


def _kv_kernel(mem_ref, g_ref, w_ref, kv_ref):
    m = _rms(mem_ref[...], g_ref[...]).astype(BF16)
    kv_ref[...] = _dot(m, w_ref[...].astype(BF16)).astype(BF16)


def _kv_call(mem, g_mem, w_kv):
    bsz, n_mem, d = mem.shape
    n_out = w_kv.shape[1]
    return pl.pallas_call(
        _kv_kernel,
        grid=(bsz,),
        in_specs=[
            pl.BlockSpec((None, n_mem, d), lambda b: (b, 0, 0)),
            pl.BlockSpec((1, d), lambda b: (0, 0)),
            pl.BlockSpec((d, n_out), lambda b: (0, 0), pipeline_mode=pl.Buffered(1)),
        ],
        out_specs=pl.BlockSpec((None, n_mem, n_out), lambda b: (b, 0, 0)),
        out_shape=jax.ShapeDtypeStruct((bsz, n_mem, n_out), BF16),
        compiler_params=pltpu.CompilerParams(
            dimension_semantics=("arbitrary",), vmem_limit_bytes=VMEM_LIMIT_BYTES),
        name="mem_kv",
    )(mem, g_mem, w_kv)


def _load_rounded(srcs, dsts, stages, sems):
    n_slots = stages[0].shape[0]
    n = srcs[0].shape[0] // stages[0].shape[1]
    assert all(s.shape[0] == n * st.shape[1] and st.shape[0] == n_slots for s, st in zip(srcs, stages))

    def rows_of(i, k):
        r = stages[i].shape[1]
        return pl.ds(k * r, r)

    def copy(i, k):
        return pltpu.make_async_copy(srcs[i].at[rows_of(i, k), :], stages[i].at[k % n_slots],
                                     sems.at[i, k % n_slots])

    for k in range(n_slots - 1):
        for i in range(len(srcs)):
            copy(i, k).start()
    for k in range(n):
        for i in range(len(srcs)):
            if k + n_slots - 1 < n:
                copy(i, k + n_slots - 1).start()
            copy(i, k).wait()
            dsts[i][rows_of(i, k), :] = stages[i][k % n_slots].astype(BF16)


def _mixer_kernel(x_ref, kv_ref, g_mix_ref, w_in_hbm, ln_g_ref, ln_b_ref, w_s_ref,
                  bias_ref, conv_w_ref, g_head_ref, w_o_hbm, w1_ref, w2_ref,
                  o_ref, w1b_ref, w2b_ref,
                  w_in_ref, w_o_ref, stage_in, stage_o, sems,
                  heads_a_ref, heads_b_ref, heads_c_ref, carry_ref, *, ts, d_sgu, d_conv, d_mem):
    s = pl.program_id(1)

    @pl.when(jnp.logical_and(pl.program_id(0) == 0, s == 0))
    def _():
        _load_rounded([w_in_hbm, w_o_hbm], [w_in_ref, w_o_ref], [stage_in, stage_o], sems)

    n_chunks = ts // CHUNK
    n_sgu_heads = d_sgu // HEAD_DIM

    @pl.when(s == 0)
    def _():
        carry_ref[...] = jnp.zeros_like(carry_ref)

    x = x_ref[...]
    h = _rms(x, g_mix_ref[...]).astype(BF16)
    o_conv = 2 * d_sgu
    o_q = o_conv + 3 * d_conv

    def head_norm(a, col):
        y = a * lax.rsqrt(jnp.mean(a * a, axis=-1, keepdims=True) + EPS)
        return (y * g_head_ref[:, col:col + HEAD_DIM]).astype(BF16)

    uv = _dot(h, w_in_ref[:, 0:2 * d_sgu])
    q = _dot(h, w_in_ref[:, o_q:o_q + d_mem]) * (HEAD_DIM ** -0.5)
    cx = _dot(h, w_in_ref[:, o_conv + d_conv:o_conv + 3 * d_conv])

    u = jax.nn.gelu(uv[:, 0:d_sgu])
    v = jax.nn.gelu(uv[:, d_sgu:2 * d_sgu])
    mu = jnp.mean(v, axis=-1, keepdims=True)
    vc = v - mu
    vn = vc * lax.rsqrt(jnp.mean(vc * vc, axis=-1, keepdims=True) + EPS)
    vn = (vn * ln_g_ref[...] + ln_b_ref[...]).astype(BF16)

    scores = []
    for hd in range(N_MEM_HEADS):
        col = hd * HEAD_DIM
        q_h = q[:, col:col + HEAD_DIM].astype(BF16)
        k_h = kv_ref[:, col:col + HEAD_DIM]
        scores.append(lax.dot_general(q_h, k_h, (((1,), (1,)), ((), ())),
                                      preferred_element_type=F32))
    b_gate = _dot(h, w_in_ref[:, o_conv:o_conv + d_conv])

    w1b_ref[...] = w1_ref[...].astype(BF16)
    w2b_ref[...] = w2_ref[...].astype(BF16)

    t_idx = lax.broadcasted_iota(jnp.int32, (CHUNK, CHUNK), 0)
    s_idx = lax.broadcasted_iota(jnp.int32, (CHUNK, CHUNK), 1)
    causal = s_idx <= t_idx
    mixed = []
    for hd in range(n_sgu_heads):
        col = hd * HEAD_DIM
        w = jnp.where(causal, w_s_ref[hd], 0.0).astype(BF16)
        v_h = jnp.concatenate(
            [vn[c * CHUNK:(c + 1) * CHUNK, col:col + HEAD_DIM] for c in range(n_chunks)], axis=1)
        mixed.append(_dot(w, v_h))

    pv = []
    for hd in range(N_MEM_HEADS):
        col = hd * HEAD_DIM
        sc = scores[hd]
        e = jnp.exp(sc - jnp.max(sc, axis=-1, keepdims=True))
        denom = jnp.sum(e, axis=-1, keepdims=True)
        v_h = kv_ref[:, d_mem + col:d_mem + col + HEAD_DIM]
        pv.append((_dot(e.astype(BF16), v_h), denom))

    xc = cx[:, 0:d_conv] * cx[:, d_conv:2 * d_conv]
    prev = carry_ref[...]
    rows = lax.broadcasted_iota(jnp.int32, (ts, d_conv), 0)
    x_m1 = jnp.where(rows == 0, prev[7:8, :], pltpu.roll(xc, 1, 0))
    x_m2 = jnp.where(rows == 0, prev[6:7, :],
                     jnp.where(rows == 1, prev[7:8, :], pltpu.roll(xc, 2, 0)))
    y = (conv_w_ref[0:1, :] * x_m2 + conv_w_ref[1:2, :] * x_m1 + conv_w_ref[2:3, :] * xc)
    c_out = b_gate * y
    carry_ref[...] = xc[ts - 8:ts, :]
    for hd in range(d_conv // HEAD_DIM):
        col = hd * HEAD_DIM
        heads_b_ref[:, col:col + HEAD_DIM] = head_norm(c_out[:, col:col + HEAD_DIM], d_sgu + col)
    out = x + _dot(heads_b_ref[...], w_o_ref[d_sgu:d_sgu + d_conv, :])

    for hd in range(n_sgu_heads):
        col = hd * HEAD_DIM
        bias = bias_ref[:, col:col + HEAD_DIM]
        for c in range(n_chunks):
            r0 = c * CHUNK
            a = u[r0:r0 + CHUNK, col:col + HEAD_DIM] * (
                mixed[hd][:, c * HEAD_DIM:(c + 1) * HEAD_DIM] + bias)
            heads_a_ref[r0:r0 + CHUNK, col:col + HEAD_DIM] = head_norm(a, col)
    out = out + _dot(heads_a_ref[...], w_o_ref[0:d_sgu, :])

    for hd in range(N_MEM_HEADS):
        col = hd * HEAD_DIM
        o_h, denom = pv[hd]
        heads_c_ref[:, col:col + HEAD_DIM] = head_norm(o_h / denom, d_sgu + d_conv + col)
    out = out + _dot(heads_c_ref[...], w_o_ref[d_sgu + d_conv:d_sgu + d_conv + d_mem, :])
    o_ref[...] = out


def _mixer_call(x, kv, g_mix, w_in, ln_g, ln_b, w_s, bias_full, conv_w, g_head, w_o, w1, w2):
    bsz, seq, d = x.shape
    d_in = w_in.shape[1]
    d_mix = w_o.shape[0]
    d_sgu = ln_g.shape[1]
    d_conv = conv_w.shape[1]
    d_mem = kv.shape[2] // 2
    n_mem = kv.shape[1]
    ts = SEQ_TILE
    assert seq % ts == 0 and ts % CHUNK == 0
    assert d_in == 2 * d_sgu + 3 * d_conv + d_mem and d_mix == d_sgu + d_conv + d_mem
    assert d % (16 * WEIGHT_CHUNKS) == 0 and d_mix % (16 * WEIGHT_CHUNKS) == 0
    n_s = seq // ts
    n_steps = bsz * n_s
    assert w1.shape[0] % (16 * n_steps) == 0 and w2.shape[0] % (16 * n_steps) == 0
    r1, r2 = w1.shape[0] // n_steps, w2.shape[0] // n_steps
    slab = lambda b, s: (b * n_s + s, 0)
    const = lambda b, s: (0, 0)
    kern = functools.partial(_mixer_kernel, ts=ts, d_sgu=d_sgu, d_conv=d_conv, d_mem=d_mem)
    return pl.pallas_call(
        kern,
        grid=(bsz, seq // ts),
        in_specs=[
            pl.BlockSpec((None, ts, d), lambda b, s: (b, s, 0)),
            pl.BlockSpec((None, n_mem, 2 * d_mem), lambda b, s: (b, 0, 0)),
            pl.BlockSpec((1, d), const),
            pl.BlockSpec(memory_space=pl.ANY),
            pl.BlockSpec((1, d_sgu), const),
            pl.BlockSpec((1, d_sgu), const),
            pl.BlockSpec(w_s.shape, lambda b, s: (0, 0, 0)),
            pl.BlockSpec((CHUNK, d_sgu), const),
            pl.BlockSpec((CONV_W, d_conv), const),
            pl.BlockSpec((1, d_mix), const),
            pl.BlockSpec(memory_space=pl.ANY),
            pl.BlockSpec((r1, w1.shape[1]), slab),
            pl.BlockSpec((r2, w2.shape[1]), slab),
        ],
        out_specs=[pl.BlockSpec((None, ts, d), lambda b, s: (b, s, 0)),
                   pl.BlockSpec((r1, w1.shape[1]), slab),
                   pl.BlockSpec((r2, w2.shape[1]), slab)],
        out_shape=[jax.ShapeDtypeStruct((bsz, seq, d), F32),
                   jax.ShapeDtypeStruct(w1.shape, BF16),
                   jax.ShapeDtypeStruct(w2.shape, BF16)],
        scratch_shapes=[
            pltpu.VMEM((d, d_in), BF16),
            pltpu.VMEM((d_mix, d), BF16),
            pltpu.VMEM((WEIGHT_STAGE_SLOTS, d // WEIGHT_CHUNKS, d_in), F32),
            pltpu.VMEM((WEIGHT_STAGE_SLOTS, d_mix // WEIGHT_CHUNKS, d), F32),
            pltpu.SemaphoreType.DMA((2, WEIGHT_STAGE_SLOTS)),
            pltpu.VMEM((ts, d_sgu), BF16),
            pltpu.VMEM((ts, d_conv), BF16),
            pltpu.VMEM((ts, d_mem), BF16),
            pltpu.VMEM((8, d_conv), F32),
        ],
        compiler_params=pltpu.CompilerParams(
            dimension_semantics=("arbitrary", "arbitrary"), vmem_limit_bytes=VMEM_LIMIT_BYTES),
        name="mixer",
    )(x, kv, g_mix, w_in, ln_g, ln_b, w_s, bias_full, conv_w, g_head, w_o, w1, w2)


def _ffn_kernel(x_ref, g_ffn_ref, w1_ref, w2_ref, g_final_ref, o_ref, hn_ref):
    j = pl.program_id(1)
    sub = w1_ref.shape[1] // FFN_SUB_TILES

    @pl.when(j == 0)
    def _():
        x = x_ref[...]
        hn_ref[...] = _rms(x, g_ffn_ref[...]).astype(BF16)
        o_ref[...] = x

    hn = hn_ref[...]
    for c in range(FFN_SUB_TILES):
        f = jnp.maximum(_dot(hn, w1_ref[:, c * sub:(c + 1) * sub]), 0.0)
        o_ref[...] += _dot((f * f).astype(BF16), w2_ref[c * sub:(c + 1) * sub, :])

    @pl.when(j == pl.num_programs(1) - 1)
    def _():
        o_ref[...] = _rms(o_ref[...], g_final_ref[...])


def _ffn_call(x, g_ffn, w1, w2, g_final):
    t, d = x.shape
    d_ff = w1.shape[1]
    tm, tf = FFN_ROW_TILE, FFN_COL_TILE
    assert t % tm == 0 and d_ff % tf == 0 and tf % (FFN_SUB_TILES * 128) == 0
    return pl.pallas_call(
        _ffn_kernel,
        grid=(t // tm, d_ff // tf),
        in_specs=[
            pl.BlockSpec((tm, d), lambda i, j: (i, 0)),
            pl.BlockSpec((1, d), lambda i, j: (0, 0)),
            pl.BlockSpec((d, tf), lambda i, j: (0, j)),
            pl.BlockSpec((tf, d), lambda i, j: (j, 0)),
            pl.BlockSpec((1, d), lambda i, j: (0, 0)),
        ],
        out_specs=pl.BlockSpec((tm, d), lambda i, j: (i, 0)),
        out_shape=jax.ShapeDtypeStruct((t, d), F32),
        scratch_shapes=[pltpu.VMEM((tm, d), BF16)],
        compiler_params=pltpu.CompilerParams(
            dimension_semantics=("arbitrary", "arbitrary"), vmem_limit_bytes=VMEM_LIMIT_BYTES),
        name="ffn",
    )(x, g_ffn, w1, w2, g_final)


def kernel(x, mem, g_mix, w_in, ln_v_g, ln_v_b, w_s, b_s, conv_w, g_mem, w_kv, g_head, w_o,
           g_ffn, w_ffn1, w_ffn2, g_final):
    bsz, seq, d = x.shape
    assert w_in.shape[0] == 1, "the FFN call fuses the final norm, so exactly one layer is supported"
    kv = _kv_call(mem, g_mem[0][None, :], w_kv[0])
    bias_full = jnp.repeat(b_s[0].T, HEAD_DIM, axis=1)
    x1, w1b, w2b = _mixer_call(x, kv, g_mix[0][None, :], w_in[0], ln_v_g[0][None, :],
                               ln_v_b[0][None, :], w_s[0], bias_full, conv_w[0], g_head[0][None, :],
                               w_o[0], w_ffn1[0], w_ffn2[0])
    y = _ffn_call(x1.reshape(bsz * seq, d), g_ffn[0][None, :], w1b, w2b, g_final[None, :])
    return y.reshape(bsz, seq, d)
```

```python
import functools

import jax
import jax.numpy as jnp
from jax import lax
from jax.experimental import pallas as pl
from jax.experimental.pallas import tpu as pltpu

F32_SUBLANES = 8
BF16_SUBLANES = 16
MXU_TILE = 256
VMEM_LIMIT_BYTES = 58 * 1024 * 1024

HEAD_DIM = 128
N_MEM_HEADS = 4
CHUNK = 128
CONV_W = 3
EPS = 1e-6

SEQ_TILE = 256
FFN_ROW_TILE = 512
FFN_COL_TILE = 2048
FFN_SUB_TILES = 4
WEIGHT_CHUNKS = 32
WEIGHT_STAGE_SLOTS = 4

F32 = jnp.float32
BF16 = jnp.bfloat16


def _rms(xf, g):
    return xf * lax.rsqrt(jnp.mean(xf * xf, axis=-1, keepdims=True) + EPS) * g


_dot = functools.partial(jnp.dot, preferred_element_type=F32)


def _kv_kernel(mem_ref, g_ref, w_ref, kv_ref):
    m = _rms(mem_ref[...], g_ref[...]).astype(BF16)
    kv_ref[...] = _dot(m, w_ref[...].astype(BF16)).astype(BF16)


def _kv_call(mem, g_mem, w_kv):
    bsz, n_mem, d = mem.shape
    n_out = w_kv.shape[1]
    return pl.pallas_call(
        _kv_kernel,
        grid=(bsz,),
        in_specs=[
            pl.BlockSpec((None, n_mem, d), lambda b: (b, 0, 0)),
            pl.BlockSpec((1, d), lambda b: (0, 0)),
            pl.BlockSpec((d, n_out), lambda b: (0, 0), pipeline_mode=pl.Buffered(1)),
        ],
        out_specs=pl.BlockSpec((None, n_mem, n_out), lambda b: (b, 0, 0)),
        out_shape=jax.ShapeDtypeStruct((bsz, n_mem, n_out), BF16),
        compiler_params=pltpu.CompilerParams(
            dimension_semantics=("arbitrary",), vmem_limit_bytes=VMEM_LIMIT_BYTES),
        name="mem_kv",
    )(mem, g_mem, w_kv)


def _load_rounded(srcs, dsts, stages, sems):
    n_slots = stages[0].shape[0]
    n = srcs[0].shape[0] // stages[0].shape[1]
    assert all(s.shape[0] == n * st.shape[1] and st.shape[0] == n_slots for s, st in zip(srcs, stages))

    def rows_of(i, k):
        r = stages[i].shape[1]
        return pl.ds(k * r, r)

    def copy(i, k):
        return pltpu.make_async_copy(srcs[i].at[rows_of(i, k), :], stages[i].at[k % n_slots],
                                     sems.at[i, k % n_slots])

    for k in range(n_slots - 1):
        for i in range(len(srcs)):
            copy(i, k).start()
    for k in range(n):
        for i in range(len(srcs)):
            if k + n_slots - 1 < n:
                copy(i, k + n_slots - 1).start()
            copy(i, k).wait()
            dsts[i][rows_of(i, k), :] = stages[i][k % n_slots].astype(BF16)


def _mixer_kernel(x_ref, kv_ref, g_mix_ref, w_in_hbm, ln_g_ref, ln_b_ref, w_s_ref,
                  bias_ref, conv_w_ref, g_head_ref, w_o_hbm, w1_ref, w2_ref,
                  o_ref, w1b_ref, w2b_ref,
                  w_in_ref, w_o_ref, stage_in, stage_o, sems,
                  heads_a_ref, heads_b_ref, heads_c_ref, carry_ref, *, ts, d_sgu, d_conv, d_mem):
    s = pl.program_id(1)

    @pl.when(jnp.logical_and(pl.program_id(0) == 0, s == 0))
    def _():
        _load_rounded([w_in_hbm, w_o_hbm], [w_in_ref, w_o_ref], [stage_in, stage_o], sems)

    n_chunks = ts // CHUNK
    n_sgu_heads = d_sgu // HEAD_DIM

    @pl.when(s == 0)
    def _():
        carry_ref[...] = jnp.zeros_like(carry_ref)

    x = x_ref[...]
    h = _rms(x, g_mix_ref[...]).astype(BF16)
    o_conv = 2 * d_sgu
    o_q = o_conv + 3 * d_conv

    def head_norm(a, col):
        y = a * lax.rsqrt(jnp.mean(a * a, axis=-1, keepdims=True) + EPS)
        return (y * g_head_ref[:, col:col + HEAD_DIM]).astype(BF16)

    uv = _dot(h, w_in_ref[:, 0:2 * d_sgu])
    q = _dot(h, w_in_ref[:, o_q:o_q + d_mem]) * (HEAD_DIM ** -0.5)
    cx = _dot(h, w_in_ref[:, o_conv + d_conv:o_conv + 3 * d_conv])

    u = jax.nn.gelu(uv[:, 0:d_sgu])
    v = jax.nn.gelu(uv[:, d_sgu:2 * d_sgu])
    mu = jnp.mean(v, axis=-1, keepdims=True)
    vc = v - mu
    vn = vc * lax.rsqrt(jnp.mean(vc * vc, axis=-1, keepdims=True) + EPS)
    vn = (vn * ln_g_ref[...] + ln_b_ref[...]).astype(BF16)

    scores = []
    for hd in range(N_MEM_HEADS):
        col = hd * HEAD_DIM
        q_h = q[:, col:col + HEAD_DIM].astype(BF16)
        k_h = kv_ref[:, col:col + HEAD_DIM]
        scores.append(lax.dot_general(q_h, k_h, (((1,), (1,)), ((), ())),
                                      preferred_element_type=F32))
    b_gate = _dot(h, w_in_ref[:, o_conv:o_conv + d_conv])

    w1b_ref[...] = w1_ref[...].astype(BF16)
    w2b_ref[...] = w2_ref[...].astype(BF16)

    t_idx = lax.broadcasted_iota(jnp.int32, (CHUNK, CHUNK), 0)
    s_idx = lax.broadcasted_iota(jnp.int32, (CHUNK, CHUNK), 1)
    causal = s_idx <= t_idx
    mixed = []
    for hd in range(n_sgu_heads):
        col = hd * HEAD_DIM
        w = jnp.where(causal, w_s_ref[hd], 0.0).astype(BF16)
        v_h = jnp.concatenate(
            [vn[c * CHUNK:(c + 1) * CHUNK, col:col + HEAD_DIM] for c in range(n_chunks)], axis=1)
        mixed.append(_dot(w, v_h))

    pv = []
    for hd in range(N_MEM_HEADS):
        col = hd * HEAD_DIM
        sc = scores[hd]
        e = jnp.exp(sc - jnp.max(sc, axis=-1, keepdims=True))
        denom = jnp.sum(e, axis=-1, keepdims=True)
        v_h = kv_ref[:, d_mem + col:d_mem + col + HEAD_DIM]
        pv.append((_dot(e.astype(BF16), v_h), denom))

    xc = cx[:, 0:d_conv] * cx[:, d_conv:2 * d_conv]
    prev = carry_ref[...]
    prev_m1 = prev[F32_SUBLANES - 1:F32_SUBLANES, :]
    prev_m2 = prev[F32_SUBLANES - 2:F32_SUBLANES - 1, :]
    rows = lax.broadcasted_iota(jnp.int32, (ts, d_conv), 0)
    x_m1 = jnp.where(rows == 0, prev_m1, pltpu.roll(xc, 1, 0))
    x_m2 = jnp.where(rows == 0, prev_m2, jnp.where(rows == 1, prev_m1, pltpu.roll(xc, 2, 0)))
    y = (conv_w_ref[0:1, :] * x_m2 + conv_w_ref[1:2, :] * x_m1 + conv_w_ref[2:3, :] * xc)
    c_out = b_gate * y
    carry_ref[...] = xc[ts - F32_SUBLANES:ts, :]
    for hd in range(d_conv // HEAD_DIM):
        col = hd * HEAD_DIM
        heads_b_ref[:, col:col + HEAD_DIM] = head_norm(c_out[:, col:col + HEAD_DIM], d_sgu + col)
    out = x + _dot(heads_b_ref[...], w_o_ref[d_sgu:d_sgu + d_conv, :])

    for hd in range(n_sgu_heads):
        col = hd * HEAD_DIM
        bias = bias_ref[:, col:col + HEAD_DIM]
        for c in range(n_chunks):
            r0 = c * CHUNK
            a = u[r0:r0 + CHUNK, col:col + HEAD_DIM] * (
                mixed[hd][:, c * HEAD_DIM:(c + 1) * HEAD_DIM] + bias)
            heads_a_ref[r0:r0 + CHUNK, col:col + HEAD_DIM] = head_norm(a, col)
    out = out + _dot(heads_a_ref[...], w_o_ref[0:d_sgu, :])

    for hd in range(N_MEM_HEADS):
        col = hd * HEAD_DIM
        o_h, denom = pv[hd]
        heads_c_ref[:, col:col + HEAD_DIM] = head_norm(o_h / denom, d_sgu + d_conv + col)
    out = out + _dot(heads_c_ref[...], w_o_ref[d_sgu + d_conv:d_sgu + d_conv + d_mem, :])
    o_ref[...] = out


def _mixer_call(x, kv, g_mix, w_in, ln_g, ln_b, w_s, bias_full, conv_w, g_head, w_o, w1, w2):
    bsz, seq, d = x.shape
    d_in = w_in.shape[1]
    d_mix = w_o.shape[0]
    d_sgu = ln_g.shape[1]
    d_conv = conv_w.shape[1]
    d_mem = kv.shape[2] // 2
    n_mem = kv.shape[1]
    ts = SEQ_TILE
    assert seq % ts == 0 and ts % CHUNK == 0
    assert d_in == 2 * d_sgu + 3 * d_conv + d_mem and d_mix == d_sgu + d_conv + d_mem
    assert d % (BF16_SUBLANES * WEIGHT_CHUNKS) == 0 and d_mix % (BF16_SUBLANES * WEIGHT_CHUNKS) == 0
    n_s = seq // ts
    n_steps = bsz * n_s
    assert w1.shape[0] % (BF16_SUBLANES * n_steps) == 0 and w2.shape[0] % (BF16_SUBLANES * n_steps) == 0
    r1, r2 = w1.shape[0] // n_steps, w2.shape[0] // n_steps
    slab = lambda b, s: (b * n_s + s, 0)
    const = lambda b, s: (0, 0)
    kern = functools.partial(_mixer_kernel, ts=ts, d_sgu=d_sgu, d_conv=d_conv, d_mem=d_mem)
    return pl.pallas_call(
        kern,
        grid=(bsz, seq // ts),
        in_specs=[
            pl.BlockSpec((None, ts, d), lambda b, s: (b, s, 0)),
            pl.BlockSpec((None, n_mem, 2 * d_mem), lambda b, s: (b, 0, 0)),
            pl.BlockSpec((1, d), const),
            pl.BlockSpec(memory_space=pl.ANY),
            pl.BlockSpec((1, d_sgu), const),
            pl.BlockSpec((1, d_sgu), const),
            pl.BlockSpec(w_s.shape, lambda b, s: (0, 0, 0)),
            pl.BlockSpec((CHUNK, d_sgu), const),
            pl.BlockSpec((CONV_W, d_conv), const),
            pl.BlockSpec((1, d_mix), const),
            pl.BlockSpec(memory_space=pl.ANY),
            pl.BlockSpec((r1, w1.shape[1]), slab),
            pl.BlockSpec((r2, w2.shape[1]), slab),
        ],
        out_specs=[pl.BlockSpec((None, ts, d), lambda b, s: (b, s, 0)),
                   pl.BlockSpec((r1, w1.shape[1]), slab),
                   pl.BlockSpec((r2, w2.shape[1]), slab)],
        out_shape=[jax.ShapeDtypeStruct((bsz, seq, d), F32),
                   jax.ShapeDtypeStruct(w1.shape, BF16),
                   jax.ShapeDtypeStruct(w2.shape, BF16)],
        scratch_shapes=[
            pltpu.VMEM((d, d_in), BF16),
            pltpu.VMEM((d_mix, d), BF16),
            pltpu.VMEM((WEIGHT_STAGE_SLOTS, d // WEIGHT_CHUNKS, d_in), F32),
            pltpu.VMEM((WEIGHT_STAGE_SLOTS, d_mix // WEIGHT_CHUNKS, d), F32),
            pltpu.SemaphoreType.DMA((2, WEIGHT_STAGE_SLOTS)),
            pltpu.VMEM((ts, d_sgu), BF16),
            pltpu.VMEM((ts, d_conv), BF16),
            pltpu.VMEM((ts, d_mem), BF16),
            pltpu.VMEM((F32_SUBLANES, d_conv), F32),
        ],
        compiler_params=pltpu.CompilerParams(
            dimension_semantics=("arbitrary", "arbitrary"), vmem_limit_bytes=VMEM_LIMIT_BYTES),
        name="mixer",
    )(x, kv, g_mix, w_in, ln_g, ln_b, w_s, bias_full, conv_w, g_head, w_o, w1, w2)


def _ffn_kernel(x_ref, g_ffn_ref, w1_ref, w2_ref, g_final_ref, o_ref, hn_ref):
    j = pl.program_id(1)
    sub = w1_ref.shape[1] // FFN_SUB_TILES

    @pl.when(j == 0)
    def _():
        x = x_ref[...]
        hn_ref[...] = _rms(x, g_ffn_ref[...]).astype(BF16)
        o_ref[...] = x

    hn = hn_ref[...]
    for c in range(FFN_SUB_TILES):
        f = jnp.maximum(_dot(hn, w1_ref[:, c * sub:(c + 1) * sub]), 0.0)
        o_ref[...] += _dot((f * f).astype(BF16), w2_ref[c * sub:(c + 1) * sub, :])

    @pl.when(j == pl.num_programs(1) - 1)
    def _():
        o_ref[...] = _rms(o_ref[...], g_final_ref[...])


def _ffn_call(x, g_ffn, w1, w2, g_final):
    t, d = x.shape
    d_ff = w1.shape[1]
    tm, tf = FFN_ROW_TILE, FFN_COL_TILE
    assert t % tm == 0 and d_ff % tf == 0 and tf % (FFN_SUB_TILES * 2 * MXU_TILE) == 0
    return pl.pallas_call(
        _ffn_kernel,
        grid=(t // tm, d_ff // tf),
        in_specs=[
            pl.BlockSpec((tm, d), lambda i, j: (i, 0)),
            pl.BlockSpec((1, d), lambda i, j: (0, 0)),
            pl.BlockSpec((d, tf), lambda i, j: (0, j)),
            pl.BlockSpec((tf, d), lambda i, j: (j, 0)),
            pl.BlockSpec((1, d), lambda i, j: (0, 0)),
        ],
        out_specs=pl.BlockSpec((tm, d), lambda i, j: (i, 0)),
        out_shape=jax.ShapeDtypeStruct((t, d), F32),
        scratch_shapes=[pltpu.VMEM((tm, d), BF16)],
        compiler_params=pltpu.CompilerParams(
            dimension_semantics=("arbitrary", "arbitrary"), vmem_limit_bytes=VMEM_LIMIT_BYTES),
        name="ffn",
    )(x, g_ffn, w1, w2, g_final)


def kernel(x, mem, g_mix, w_in, ln_v_g, ln_v_b, w_s, b_s, conv_w, g_mem, w_kv, g_head, w_o,
           g_ffn, w_ffn1, w_ffn2, g_final):
    bsz, seq, d = x.shape
    assert w_in.shape[0] == 1, "the FFN call fuses the final norm, so exactly one layer is supported"
    kv = _kv_call(mem, g_mem[0][None, :], w_kv[0])
    bias_full = jnp.repeat(b_s[0].T, HEAD_DIM, axis=1)
    x1, w1b, w2b = _mixer_call(x, kv, g_mix[0][None, :], w_in[0], ln_v_g[0][None, :],
                               ln_v_b[0][None, :], w_s[0], bias_full, conv_w[0], g_head[0][None, :],
                               w_o[0], w_ffn1[0], w_ffn2[0])
    y = _ffn_call(x1.reshape(bsz * seq, d), g_ffn[0][None, :], w1b, w2b, g_final[None, :])
    return y.reshape(bsz, seq, d)
```

```python
import functools

import jax
import jax.numpy as jnp
from jax import lax
from jax.experimental import pallas as pl
from jax.experimental.pallas import tpu as pltpu

F32_SUBLANES = 8
BF16_SUBLANES = 16
MXU_TILE = 256
VMEM_LIMIT_BYTES = 58 * 1024 * 1024

HEAD_DIM = 128
N_MEM_HEADS = 4
CHUNK = 128
CONV_W = 3
EPS = 1e-6

SEQ_TILE = 256
FFN_ROW_TILE = 512
FFN_COL_TILE = 2048
FFN_SUB_TILES = 4
WEIGHT_CHUNKS = 32
WEIGHT_STAGE_SLOTS = 4

F32 = jnp.float32
BF16 = jnp.bfloat16


def _rms(xf, g):
    return xf * lax.rsqrt(jnp.mean(xf * xf, axis=-1, keepdims=True) + EPS) * g


_dot = functools.partial(jnp.dot, preferred_element_type=F32)


def _kv_kernel(mem_ref, g_ref, w_ref, kv_ref):
    m = _rms(mem_ref[...], g_ref[...]).astype(BF16)
    kv_ref[...] = _dot(m, w_ref[...].astype(BF16)).astype(BF16)


def _kv_call(mem, g_mem, w_kv):
    bsz, n_mem, d = mem.shape
    n_out = w_kv.shape[1]
    return pl.pallas_call(
        _kv_kernel,
        grid=(bsz,),
        in_specs=[
            pl.BlockSpec((None, n_mem, d), lambda b: (b, 0, 0)),
            pl.BlockSpec((1, d), lambda b: (0, 0)),
            pl.BlockSpec((d, n_out), lambda b: (0, 0), pipeline_mode=pl.Buffered(1)),
        ],
        out_specs=pl.BlockSpec((None, n_mem, n_out), lambda b: (b, 0, 0)),
        out_shape=jax.ShapeDtypeStruct((bsz, n_mem, n_out), BF16),
        compiler_params=pltpu.CompilerParams(
            dimension_semantics=("arbitrary",), vmem_limit_bytes=VMEM_LIMIT_BYTES),
        name="mem_kv",
    )(mem, g_mem, w_kv)


def _load_rounded(srcs, dsts, stages, sems):
    n_slots = stages[0].shape[0]
    n = srcs[0].shape[0] // stages[0].shape[1]
    assert all(s.shape[0] == n * st.shape[1] and st.shape[0] == n_slots for s, st in zip(srcs, stages))

    def rows_of(i, k):
        r = stages[i].shape[1]
        return pl.ds(k * r, r)

    def copy(i, k):
        return pltpu.make_async_copy(srcs[i].at[rows_of(i, k), :], stages[i].at[k % n_slots],
                                     sems.at[i, k % n_slots])

    for k in range(n_slots - 1):
        for i in range(len(srcs)):
            copy(i, k).start()
    for k in range(n):
        for i in range(len(srcs)):
            if k + n_slots - 1 < n:
                copy(i, k + n_slots - 1).start()
            copy(i, k).wait()
            dsts[i][rows_of(i, k), :] = stages[i][k % n_slots].astype(BF16)


def _mixer_kernel(x_ref, kv_ref, g_mix_ref, w_in_hbm, ln_g_ref, ln_b_ref, w_s_ref,
                  bias_ref, conv_w_ref, g_head_ref, w_o_hbm, w1_ref, w2_ref,
                  o_ref, w1b_ref, w2b_ref,
                  w_in_ref, w_o_ref, stage_in, stage_o, sems,
                  heads_a_ref, heads_b_ref, heads_c_ref, carry_ref, *, ts, d_sgu, d_conv, d_mem):
    s = pl.program_id(1)

    @pl.when(jnp.logical_and(pl.program_id(0) == 0, s == 0))
    def _():
        _load_rounded([w_in_hbm, w_o_hbm], [w_in_ref, w_o_ref], [stage_in, stage_o], sems)

    n_chunks = ts // CHUNK
    n_sgu_heads = d_sgu // HEAD_DIM

    @pl.when(s == 0)
    def _():
        carry_ref[...] = jnp.zeros_like(carry_ref)

    x = x_ref[...]
    h = _rms(x, g_mix_ref[...]).astype(BF16)
    o_conv = 2 * d_sgu
    o_q = o_conv + 3 * d_conv

    def head_norm(a, col):
        y = a * lax.rsqrt(jnp.mean(a * a, axis=-1, keepdims=True) + EPS)
        return (y * g_head_ref[:, col:col + HEAD_DIM]).astype(BF16)

    uv = _dot(h, w_in_ref[:, 0:2 * d_sgu])
    q = _dot(h, w_in_ref[:, o_q:o_q + d_mem]) * (HEAD_DIM ** -0.5)
    cx = _dot(h, w_in_ref[:, o_conv + d_conv:o_conv + 3 * d_conv])

    u = jax.nn.gelu(uv[:, 0:d_sgu])
    v = jax.nn.gelu(uv[:, d_sgu:2 * d_sgu])
    mu = jnp.mean(v, axis=-1, keepdims=True)
    vc = v - mu
    vn = vc * lax.rsqrt(jnp.mean(vc * vc, axis=-1, keepdims=True) + EPS)
    vn = (vn * ln_g_ref[...] + ln_b_ref[...]).astype(BF16)

    scores = []
    for hd in range(N_MEM_HEADS):
        col = hd * HEAD_DIM
        q_h = q[:, col:col + HEAD_DIM].astype(BF16)
        k_h = kv_ref[:, col:col + HEAD_DIM]
        scores.append(lax.dot_general(q_h, k_h, (((1,), (1,)), ((), ())),
                                      preferred_element_type=F32))
    b_gate = _dot(h, w_in_ref[:, o_conv:o_conv + d_conv])

    w1b_ref[...] = w1_ref[...].astype(BF16)
    w2b_ref[...] = w2_ref[...].astype(BF16)

    t_idx = lax.broadcasted_iota(jnp.int32, (CHUNK, CHUNK), 0)
    s_idx = lax.broadcasted_iota(jnp.int32, (CHUNK, CHUNK), 1)
    causal = s_idx <= t_idx
    mixed = []
    for hd in range(n_sgu_heads):
        col = hd * HEAD_DIM
        w = jnp.where(causal, w_s_ref[hd], 0.0).astype(BF16)
        v_h = jnp.concatenate(
            [vn[c * CHUNK:(c + 1) * CHUNK, col:col + HEAD_DIM] for c in range(n_chunks)], axis=1)
        mixed.append(_dot(w, v_h))

    pv = []
    for hd in range(N_MEM_HEADS):
        col = hd * HEAD_DIM
        sc = scores[hd]
        e = jnp.exp(sc - jnp.max(sc, axis=-1, keepdims=True))
        denom = jnp.sum(e, axis=-1, keepdims=True)
        v_h = kv_ref[:, d_mem + col:d_mem + col + HEAD_DIM]
        pv.append((_dot(e.astype(BF16), v_h), denom))

    xc = cx[:, 0:d_conv] * cx[:, d_conv:2 * d_conv]
    prev = carry_ref[...]
    prev_m1 = prev[F32_SUBLANES - 1:F32_SUBLANES, :]
    prev_m2 = prev[F32_SUBLANES - 2:F32_SUBLANES - 1, :]
    rows = lax.broadcasted_iota(jnp.int32, (ts, d_conv), 0)
    x_m1 = jnp.where(rows == 0, prev_m1, pltpu.roll(xc, 1, 0))
    x_m2 = jnp.where(rows == 0, prev_m2, jnp.where(rows == 1, prev_m1, pltpu.roll(xc, 2, 0)))
    y = (conv_w_ref[0:1, :] * x_m2 + conv_w_ref[1:2, :] * x_m1 + conv_w_ref[2:3, :] * xc)
    c_out = b_gate * y
    carry_ref[...] = xc[ts - F32_SUBLANES:ts, :]
    for hd in range(d_conv // HEAD_DIM):
        col = hd * HEAD_DIM
        heads_b_ref[:, col:col + HEAD_DIM] = head_norm(c_out[:, col:col + HEAD_DIM], d_sgu + col)
    out = x + _dot(heads_b_ref[...], w_o_ref[d_sgu:d_sgu + d_conv, :])

    for hd in range(n_sgu_heads):
        col = hd * HEAD_DIM
        bias = bias_ref[:, col:col + HEAD_DIM]
        for c in range(n_chunks):
            r0 = c * CHUNK
            a = u[r0:r0 + CHUNK, col:col + HEAD_DIM] * (
                mixed[hd][:, c * HEAD_DIM:(c + 1) * HEAD_DIM] + bias)
            heads_a_ref[r0:r0 + CHUNK, col:col + HEAD_DIM] = head_norm(a, col)
    out = out + _dot(heads_a_ref[...], w_o_ref[0:d_sgu, :])

    for hd in range(N_MEM_HEADS):
        col = hd * HEAD_DIM
        o_h, denom = pv[hd]
        heads_c_ref[:, col:col + HEAD_DIM] = head_norm(o_h / denom, d_sgu + d_conv + col)
    out = out + _dot(heads_c_ref[...], w_o_ref[d_sgu + d_conv:d_sgu + d_conv + d_mem, :])
    o_ref[...] = out


def _mixer_call(x, kv, g_mix, w_in, ln_g, ln_b, w_s, bias_full, conv_w, g_head, w_o, w1, w2):
    bsz, seq, d = x.shape
    d_in = w_in.shape[1]
    d_mix = w_o.shape[0]
    d_sgu = ln_g.shape[1]
    d_conv = conv_w.shape[1]
    d_mem = kv.shape[2] // 2
    n_mem = kv.shape[1]
    ts = SEQ_TILE
    assert seq % ts == 0 and ts % CHUNK == 0
    assert d_in == 2 * d_sgu + 3 * d_conv + d_mem and d_mix == d_sgu + d_conv + d_mem
    assert d % (BF16_SUBLANES * WEIGHT_CHUNKS) == 0 and d_mix % (BF16_SUBLANES * WEIGHT_CHUNKS) == 0
    n_s = seq // ts
    n_steps = bsz * n_s
    assert w1.shape[0] % (BF16_SUBLANES * n_steps) == 0 and w2.shape[0] % (BF16_SUBLANES * n_steps) == 0
    r1, r2 = w1.shape[0] // n_steps, w2.shape[0] // n_steps
    slab = lambda b, s: (b * n_s + s, 0)
    const = lambda b, s: (0, 0)
    kern = functools.partial(_mixer_kernel, ts=ts, d_sgu=d_sgu, d_conv=d_conv, d_mem=d_mem)
    return pl.pallas_call(
        kern,
        grid=(bsz, seq // ts),
        in_specs=[
            pl.BlockSpec((None, ts, d), lambda b, s: (b, s, 0)),
            pl.BlockSpec((None, n_mem, 2 * d_mem), lambda b, s: (b, 0, 0)),
            pl.BlockSpec((1, d), const),
            pl.BlockSpec(memory_space=pl.ANY),
            pl.BlockSpec((1, d_sgu), const),
            pl.BlockSpec((1, d_sgu), const),
            pl.BlockSpec(w_s.shape, lambda b, s: (0, 0, 0)),
            pl.BlockSpec((CHUNK, d_sgu), const),
            pl.BlockSpec((CONV_W, d_conv), const),
            pl.BlockSpec((1, d_mix), const),
            pl.BlockSpec(memory_space=pl.ANY),
            pl.BlockSpec((r1, w1.shape[1]), slab),
            pl.BlockSpec((r2, w2.shape[1]), slab),
        ],
        out_specs=[pl.BlockSpec((None, ts, d), lambda b, s: (b, s, 0)),
                   pl.BlockSpec((r1, w1.shape[1]), slab),
                   pl.BlockSpec((r2, w2.shape[1]), slab)],
        out_shape=[jax.ShapeDtypeStruct((bsz, seq, d), F32),
                   jax.ShapeDtypeStruct(w1.shape, BF16),
                   jax.ShapeDtypeStruct(w2.shape, BF16)],
        scratch_shapes=[
            pltpu.VMEM((d, d_in), BF16),
            pltpu.VMEM((d_mix, d), BF16),
            pltpu.VMEM((WEIGHT_STAGE_SLOTS, d // WEIGHT_CHUNKS, d_in), F32),
            pltpu.VMEM((WEIGHT_STAGE_SLOTS, d_mix // WEIGHT_CHUNKS, d), F32),
            pltpu.SemaphoreType.DMA((2, WEIGHT_STAGE_SLOTS)),
            pltpu.VMEM((ts, d_sgu), BF16),
            pltpu.VMEM((ts, d_conv), BF16),
            pltpu.VMEM((ts, d_mem), BF16),
            pltpu.VMEM((F32_SUBLANES, d_conv), F32),
        ],
        compiler_params=pltpu.CompilerParams(
            dimension_semantics=("arbitrary", "arbitrary"), vmem_limit_bytes=VMEM_LIMIT_BYTES),
        name="mixer",
    )(x, kv, g_mix, w_in, ln_g, ln_b, w_s, bias_full, conv_w, g_head, w_o, w1, w2)


def _ffn_kernel(x_ref, g_ffn_ref, w1_ref, w2_ref, g_final_ref, o_ref, hn_ref):
    j = pl.program_id(1)
    tm = x_ref.shape[0]
    sub = w1_ref.shape[1] // FFN_SUB_TILES

    def slice_matmuls(c, rows, hn):
        f = jnp.maximum(_dot(hn, w1_ref[:, c * sub:(c + 1) * sub]), 0.0)
        o_ref[rows, :] += _dot((f * f).astype(BF16), w2_ref[c * sub:(c + 1) * sub, :])

    @pl.when(j == 0)
    def _():
        for g in range(2):
            rows = pl.ds(g * (tm // 2), tm // 2)
            x = x_ref[rows, :]
            hn = _rms(x, g_ffn_ref[...]).astype(BF16)
            hn_ref[rows, :] = hn
            o_ref[rows, :] = x
            slice_matmuls(0, rows, hn)
        hn = hn_ref[...]
        for c in range(1, FFN_SUB_TILES):
            slice_matmuls(c, pl.ds(0, tm), hn)

    @pl.when(j > 0)
    def _():
        hn = hn_ref[...]
        for c in range(FFN_SUB_TILES):
            slice_matmuls(c, pl.ds(0, tm), hn)

    @pl.when(j == pl.num_programs(1) - 1)
    def _():
        o_ref[...] = _rms(o_ref[...], g_final_ref[...])


def _ffn_call(x, g_ffn, w1, w2, g_final):
    t, d = x.shape
    d_ff = w1.shape[1]
    tm, tf = FFN_ROW_TILE, FFN_COL_TILE
    assert t % tm == 0 and d_ff % tf == 0 and tf % (FFN_SUB_TILES * 2 * MXU_TILE) == 0
    return pl.pallas_call(
        _ffn_kernel,
        grid=(t // tm, d_ff // tf),
        in_specs=[
            pl.BlockSpec((tm, d), lambda i, j: (i, 0)),
            pl.BlockSpec((1, d), lambda i, j: (0, 0)),
            pl.BlockSpec((d, tf), lambda i, j: (0, j)),
            pl.BlockSpec((tf, d), lambda i, j: (j, 0)),
            pl.BlockSpec((1, d), lambda i, j: (0, 0)),
        ],
        out_specs=pl.BlockSpec((tm, d), lambda i, j: (i, 0)),
        out_shape=jax.ShapeDtypeStruct((t, d), F32),
        scratch_shapes=[pltpu.VMEM((tm, d), BF16)],
        compiler_params=pltpu.CompilerParams(
            dimension_semantics=("arbitrary", "arbitrary"), vmem_limit_bytes=VMEM_LIMIT_BYTES),
        name="ffn",
    )(x, g_ffn, w1, w2, g_final)


def kernel(x, mem, g_mix, w_in, ln_v_g, ln_v_b, w_s, b_s, conv_w, g_mem, w_kv, g_head, w_o,
           g_ffn, w_ffn1, w_ffn2, g_final):
    bsz, seq, d = x.shape
    assert w_in.shape[0] == 1, "the FFN call fuses the final norm, so exactly one layer is supported"
    kv = _kv_call(mem, g_mem[0][None, :], w_kv[0])
    bias_full = jnp.repeat(b_s[0].T, HEAD_DIM, axis=1)
    x1, w1b, w2b = _mixer_call(x, kv, g_mix[0][None, :], w_in[0], ln_v_g[0][None, :],
                               ln_v_b[0][None, :], w_s[0], bias_full, conv_w[0], g_head[0][None, :],
                               w_o[0], w_ffn1[0], w_ffn2[0])
    y = _ffn_call(x1.reshape(bsz * seq, d), g_ffn[0][None, :], w1b, w2b, g_final[None, :])
    return y.reshape(bsz, seq, d)
```

```python
import functools

import jax
import jax.numpy as jnp
from jax import lax
from jax.experimental import pallas as pl
from jax.experimental.pallas import tpu as pltpu

F32_SUBLANES = 8
BF16_SUBLANES = 16
MXU_TILE = 256
VMEM_LIMIT_BYTES = 58 * 1024 * 1024

HEAD_DIM = 128
N_MEM_HEADS = 4
CHUNK = 128
CONV_W = 3
EPS = 1e-6

SEQ_TILE = 256
FFN_ROW_TILE = 512
FFN_COL_TILE = 2048
FFN_SUB_TILES = 2
WEIGHT_CHUNKS = 32
WEIGHT_STAGE_SLOTS = 4

F32 = jnp.float32
BF16 = jnp.bfloat16


def _rms(xf, g):
    return xf * lax.rsqrt(jnp.mean(xf * xf, axis=-1, keepdims=True) + EPS) * g


_dot = functools.partial(jnp.dot, preferred_element_type=F32)


def _kv_kernel(mem_ref, g_ref, w_ref, kv_ref):
    m = _rms(mem_ref[...], g_ref[...]).astype(BF16)
    kv_ref[...] = _dot(m, w_ref[...].astype(BF16)).astype(BF16)


def _kv_call(mem, g_mem, w_kv):
    bsz, n_mem, d = mem.shape
    n_out = w_kv.shape[1]
    return pl.pallas_call(
        _kv_kernel,
        grid=(bsz,),
        in_specs=[
            pl.BlockSpec((None, n_mem, d), lambda b: (b, 0, 0)),
            pl.BlockSpec((1, d), lambda b: (0, 0)),
            pl.BlockSpec((d, n_out), lambda b: (0, 0), pipeline_mode=pl.Buffered(1)),
        ],
        out_specs=pl.BlockSpec((None, n_mem, n_out), lambda b: (b, 0, 0)),
        out_shape=jax.ShapeDtypeStruct((bsz, n_mem, n_out), BF16),
        compiler_params=pltpu.CompilerParams(
            dimension_semantics=("arbitrary",), vmem_limit_bytes=VMEM_LIMIT_BYTES),
        name="mem_kv",
    )(mem, g_mem, w_kv)


def _load_rounded(srcs, dsts, stages, sems):
    n_slots = stages[0].shape[0]
    n = srcs[0].shape[0] // stages[0].shape[1]
    assert all(s.shape[0] == n * st.shape[1] and st.shape[0] == n_slots for s, st in zip(srcs, stages))

    def rows_of(i, k):
        r = stages[i].shape[1]
        return pl.ds(k * r, r)

    def copy(i, k):
        return pltpu.make_async_copy(srcs[i].at[rows_of(i, k), :], stages[i].at[k % n_slots],
                                     sems.at[i, k % n_slots])

    for k in range(n_slots - 1):
        for i in range(len(srcs)):
            copy(i, k).start()
    for k in range(n):
        for i in range(len(srcs)):
            if k + n_slots - 1 < n:
                copy(i, k + n_slots - 1).start()
            copy(i, k).wait()
            dsts[i][rows_of(i, k), :] = stages[i][k % n_slots].astype(BF16)


def _mixer_kernel(x_ref, kv_ref, g_mix_ref, w_in_hbm, ln_g_ref, ln_b_ref, w_s_ref,
                  bias_ref, conv_w_ref, g_head_ref, w_o_hbm, w1_ref, w2_ref,
                  o_ref, w1b_ref, w2b_ref,
                  w_in_ref, w_o_ref, stage_in, stage_o, sems,
                  heads_a_ref, heads_b_ref, heads_c_ref, carry_ref, *, ts, d_sgu, d_conv, d_mem):
    s = pl.program_id(1)

    @pl.when(jnp.logical_and(pl.program_id(0) == 0, s == 0))
    def _():
        _load_rounded([w_in_hbm, w_o_hbm], [w_in_ref, w_o_ref], [stage_in, stage_o], sems)

    n_chunks = ts // CHUNK
    n_sgu_heads = d_sgu // HEAD_DIM

    @pl.when(s == 0)
    def _():
        carry_ref[...] = jnp.zeros_like(carry_ref)

    x = x_ref[...]
    h = _rms(x, g_mix_ref[...]).astype(BF16)
    o_conv = 2 * d_sgu
    o_q = o_conv + 3 * d_conv

    def head_norm(a, col):
        y = a * lax.rsqrt(jnp.mean(a * a, axis=-1, keepdims=True) + EPS)
        return (y * g_head_ref[:, col:col + HEAD_DIM]).astype(BF16)

    uv = _dot(h, w_in_ref[:, 0:2 * d_sgu])
    q = _dot(h, w_in_ref[:, o_q:o_q + d_mem]) * (HEAD_DIM ** -0.5)
    cx = _dot(h, w_in_ref[:, o_conv + d_conv:o_conv + 3 * d_conv])

    u = jax.nn.gelu(uv[:, 0:d_sgu])
    v = jax.nn.gelu(uv[:, d_sgu:2 * d_sgu])
    mu = jnp.mean(v, axis=-1, keepdims=True)
    vc = v - mu
    vn = vc * lax.rsqrt(jnp.mean(vc * vc, axis=-1, keepdims=True) + EPS)
    vn = (vn * ln_g_ref[...] + ln_b_ref[...]).astype(BF16)

    scores = []
    for hd in range(N_MEM_HEADS):
        col = hd * HEAD_DIM
        q_h = q[:, col:col + HEAD_DIM].astype(BF16)
        k_h = kv_ref[:, col:col + HEAD_DIM]
        scores.append(lax.dot_general(q_h, k_h, (((1,), (1,)), ((), ())),
                                      preferred_element_type=F32))
    b_gate = _dot(h, w_in_ref[:, o_conv:o_conv + d_conv])

    w1b_ref[...] = w1_ref[...].astype(BF16)
    w2b_ref[...] = w2_ref[...].astype(BF16)

    t_idx = lax.broadcasted_iota(jnp.int32, (CHUNK, CHUNK), 0)
    s_idx = lax.broadcasted_iota(jnp.int32, (CHUNK, CHUNK), 1)
    causal = s_idx <= t_idx
    mixed = []
    for hd in range(n_sgu_heads):
        col = hd * HEAD_DIM
        w = jnp.where(causal, w_s_ref[hd], 0.0).astype(BF16)
        v_h = jnp.concatenate(
            [vn[c * CHUNK:(c + 1) * CHUNK, col:col + HEAD_DIM] for c in range(n_chunks)], axis=1)
        mixed.append(_dot(w, v_h))

    pv = []
    for hd in range(N_MEM_HEADS):
        col = hd * HEAD_DIM
        sc = scores[hd]
        e = jnp.exp(sc - jnp.max(sc, axis=-1, keepdims=True))
        denom = jnp.sum(e, axis=-1, keepdims=True)
        v_h = kv_ref[:, d_mem + col:d_mem + col + HEAD_DIM]
        pv.append((_dot(e.astype(BF16), v_h), denom))

    xc = cx[:, 0:d_conv] * cx[:, d_conv:2 * d_conv]
    prev = carry_ref[...]
    prev_m1 = prev[F32_SUBLANES - 1:F32_SUBLANES, :]
    prev_m2 = prev[F32_SUBLANES - 2:F32_SUBLANES - 1, :]
    rows = lax.broadcasted_iota(jnp.int32, (ts, d_conv), 0)
    x_m1 = jnp.where(rows == 0, prev_m1, pltpu.roll(xc, 1, 0))
    x_m2 = jnp.where(rows == 0, prev_m2, jnp.where(rows == 1, prev_m1, pltpu.roll(xc, 2, 0)))
    y = (conv_w_ref[0:1, :] * x_m2 + conv_w_ref[1:2, :] * x_m1 + conv_w_ref[2:3, :] * xc)
    c_out = b_gate * y
    carry_ref[...] = xc[ts - F32_SUBLANES:ts, :]
    for hd in range(d_conv // HEAD_DIM):
        col = hd * HEAD_DIM
        heads_b_ref[:, col:col + HEAD_DIM] = head_norm(c_out[:, col:col + HEAD_DIM], d_sgu + col)
    out = x + _dot(heads_b_ref[...], w_o_ref[d_sgu:d_sgu + d_conv, :])

    for hd in range(n_sgu_heads):
        col = hd * HEAD_DIM
        bias = bias_ref[:, col:col + HEAD_DIM]
        for c in range(n_chunks):
            r0 = c * CHUNK
            a = u[r0:r0 + CHUNK, col:col + HEAD_DIM] * (
                mixed[hd][:, c * HEAD_DIM:(c + 1) * HEAD_DIM] + bias)
            heads_a_ref[r0:r0 + CHUNK, col:col + HEAD_DIM] = head_norm(a, col)
    out = out + _dot(heads_a_ref[...], w_o_ref[0:d_sgu, :])

    for hd in range(N_MEM_HEADS):
        col = hd * HEAD_DIM
        o_h, denom = pv[hd]
        heads_c_ref[:, col:col + HEAD_DIM] = head_norm(o_h / denom, d_sgu + d_conv + col)
    out = out + _dot(heads_c_ref[...], w_o_ref[d_sgu + d_conv:d_sgu + d_conv + d_mem, :])
    o_ref[...] = out


def _mixer_call(x, kv, g_mix, w_in, ln_g, ln_b, w_s, bias_full, conv_w, g_head, w_o, w1, w2):
    bsz, seq, d = x.shape
    d_in = w_in.shape[1]
    d_mix = w_o.shape[0]
    d_sgu = ln_g.shape[1]
    d_conv = conv_w.shape[1]
    d_mem = kv.shape[2] // 2
    n_mem = kv.shape[1]
    ts = SEQ_TILE
    assert seq % ts == 0 and ts % CHUNK == 0
    assert d_in == 2 * d_sgu + 3 * d_conv + d_mem and d_mix == d_sgu + d_conv + d_mem
    assert d % (BF16_SUBLANES * WEIGHT_CHUNKS) == 0 and d_mix % (BF16_SUBLANES * WEIGHT_CHUNKS) == 0
    n_s = seq // ts
    n_steps = bsz * n_s
    assert w1.shape[0] % (BF16_SUBLANES * n_steps) == 0 and w2.shape[0] % (BF16_SUBLANES * n_steps) == 0
    r1, r2 = w1.shape[0] // n_steps, w2.shape[0] // n_steps
    slab = lambda b, s: (b * n_s + s, 0)
    const = lambda b, s: (0, 0)
    kern = functools.partial(_mixer_kernel, ts=ts, d_sgu=d_sgu, d_conv=d_conv, d_mem=d_mem)
    return pl.pallas_call(
        kern,
        grid=(bsz, seq // ts),
        in_specs=[
            pl.BlockSpec((None, ts, d), lambda b, s: (b, s, 0)),
            pl.BlockSpec((None, n_mem, 2 * d_mem), lambda b, s: (b, 0, 0)),
            pl.BlockSpec((1, d), const),
            pl.BlockSpec(memory_space=pl.ANY),
            pl.BlockSpec((1, d_sgu), const),
            pl.BlockSpec((1, d_sgu), const),
            pl.BlockSpec(w_s.shape, lambda b, s: (0, 0, 0)),
            pl.BlockSpec((CHUNK, d_sgu), const),
            pl.BlockSpec((CONV_W, d_conv), const),
            pl.BlockSpec((1, d_mix), const),
            pl.BlockSpec(memory_space=pl.ANY),
            pl.BlockSpec((r1, w1.shape[1]), slab),
            pl.BlockSpec((r2, w2.shape[1]), slab),
        ],
        out_specs=[pl.BlockSpec((None, ts, d), lambda b, s: (b, s, 0)),
                   pl.BlockSpec((r1, w1.shape[1]), slab),
                   pl.BlockSpec((r2, w2.shape[1]), slab)],
        out_shape=[jax.ShapeDtypeStruct((bsz, seq, d), F32),
                   jax.ShapeDtypeStruct(w1.shape, BF16),
                   jax.ShapeDtypeStruct(w2.shape, BF16)],
        scratch_shapes=[
            pltpu.VMEM((d, d_in), BF16),
            pltpu.VMEM((d_mix, d), BF16),
            pltpu.VMEM((WEIGHT_STAGE_SLOTS, d // WEIGHT_CHUNKS, d_in), F32),
            pltpu.VMEM((WEIGHT_STAGE_SLOTS, d_mix // WEIGHT_CHUNKS, d), F32),
            pltpu.SemaphoreType.DMA((2, WEIGHT_STAGE_SLOTS)),
            pltpu.VMEM((ts, d_sgu), BF16),
            pltpu.VMEM((ts, d_conv), BF16),
            pltpu.VMEM((ts, d_mem), BF16),
            pltpu.VMEM((F32_SUBLANES, d_conv), F32),
        ],
        compiler_params=pltpu.CompilerParams(
            dimension_semantics=("arbitrary", "arbitrary"), vmem_limit_bytes=VMEM_LIMIT_BYTES),
        name="mixer",
    )(x, kv, g_mix, w_in, ln_g, ln_b, w_s, bias_full, conv_w, g_head, w_o, w1, w2)


def _ffn_kernel(x_ref, g_ffn_ref, w1_ref, w2_ref, g_final_ref, o_ref, hn_ref):
    j = pl.program_id(1)
    sub = w1_ref.shape[1] // FFN_SUB_TILES

    @pl.when(j == 0)
    def _():
        x = x_ref[...]
        hn_ref[...] = _rms(x, g_ffn_ref[...]).astype(BF16)
        o_ref[...] = x

    hn = hn_ref[...]
    for c in range(FFN_SUB_TILES):
        f = jnp.maximum(_dot(hn, w1_ref[:, c * sub:(c + 1) * sub]), 0.0)
        o_ref[...] += _dot((f * f).astype(BF16), w2_ref[c * sub:(c + 1) * sub, :])

    @pl.when(j == pl.num_programs(1) - 1)
    def _():
        o_ref[...] = _rms(o_ref[...], g_final_ref[...])


def _ffn_call(x, g_ffn, w1, w2, g_final):
    t, d = x.shape
    d_ff = w1.shape[1]
    tm, tf = FFN_ROW_TILE, FFN_COL_TILE
    assert t % tm == 0 and d_ff % tf == 0 and tf % (FFN_SUB_TILES * 2 * MXU_TILE) == 0
    return pl.pallas_call(
        _ffn_kernel,
        grid=(t // tm, d_ff // tf),
        in_specs=[
            pl.BlockSpec((tm, d), lambda i, j: (i, 0)),
            pl.BlockSpec((1, d), lambda i, j: (0, 0)),
            pl.BlockSpec((d, tf), lambda i, j: (0, j)),
            pl.BlockSpec((tf, d), lambda i, j: (j, 0)),
            pl.BlockSpec((1, d), lambda i, j: (0, 0)),
        ],
        out_specs=pl.BlockSpec((tm, d), lambda i, j: (i, 0)),
        out_shape=jax.ShapeDtypeStruct((t, d), F32),
        scratch_shapes=[pltpu.VMEM((tm, d), BF16)],
        compiler_params=pltpu.CompilerParams(
            dimension_semantics=("arbitrary", "arbitrary"), vmem_limit_bytes=VMEM_LIMIT_BYTES),
        name="ffn",
    )(x, g_ffn, w1, w2, g_final)


def kernel(x, mem, g_mix, w_in, ln_v_g, ln_v_b, w_s, b_s, conv_w, g_mem, w_kv, g_head, w_o,
           g_ffn, w_ffn1, w_ffn2, g_final):
    bsz, seq, d = x.shape
    assert w_in.shape[0] == 1, "the FFN call fuses the final norm, so exactly one layer is supported"
    kv = _kv_call(mem, g_mem[0][None, :], w_kv[0])
    bias_full = jnp.repeat(b_s[0].T, HEAD_DIM, axis=1)
    x1, w1b, w2b = _mixer_call(x, kv, g_mix[0][None, :], w_in[0], ln_v_g[0][None, :],
                               ln_v_b[0][None, :], w_s[0], bias_full, conv_w[0], g_head[0][None, :],
                               w_o[0], w_ffn1[0], w_ffn2[0])
    y = _ffn_call(x1.reshape(bsz * seq, d), g_ffn[0][None, :], w1b, w2b, g_final[None, :])
    return y.reshape(bsz, seq, d)
```

```python
import functools

import jax
import jax.numpy as jnp
from jax import lax
from jax.experimental import pallas as pl
from jax.experimental.pallas import tpu as pltpu

F32_SUBLANES = 8
BF16_SUBLANES = 16
MXU_TILE = 256
VMEM_LIMIT_BYTES = 58 * 1024 * 1024

HEAD_DIM = 128
N_MEM_HEADS = 4
CHUNK = 128
CONV_W = 3
EPS = 1e-6

SEQ_TILE = 256
FFN_ROW_TILE = 512
FFN_COL_TILE = 2048
FFN_SUB_TILES = 1
WEIGHT_CHUNKS = 32
WEIGHT_STAGE_SLOTS = 4

F32 = jnp.float32
BF16 = jnp.bfloat16


def _rms(xf, g):
    return xf * lax.rsqrt(jnp.mean(xf * xf, axis=-1, keepdims=True) + EPS) * g


_dot = functools.partial(jnp.dot, preferred_element_type=F32)


def _kv_kernel(mem_ref, g_ref, w_ref, kv_ref):
    m = _rms(mem_ref[...], g_ref[...]).astype(BF16)
    kv_ref[...] = _dot(m, w_ref[...].astype(BF16)).astype(BF16)


def _kv_call(mem, g_mem, w_kv):
    bsz, n_mem, d = mem.shape
    n_out = w_kv.shape[1]
    return pl.pallas_call(
        _kv_kernel,
        grid=(bsz,),
        in_specs=[
            pl.BlockSpec((None, n_mem, d), lambda b: (b, 0, 0)),
            pl.BlockSpec((1, d), lambda b: (0, 0)),
            pl.BlockSpec((d, n_out), lambda b: (0, 0), pipeline_mode=pl.Buffered(1)),
        ],
        out_specs=pl.BlockSpec((None, n_mem, n_out), lambda b: (b, 0, 0)),
        out_shape=jax.ShapeDtypeStruct((bsz, n_mem, n_out), BF16),
        compiler_params=pltpu.CompilerParams(
            dimension_semantics=("arbitrary",), vmem_limit_bytes=VMEM_LIMIT_BYTES),
        name="mem_kv",
    )(mem, g_mem, w_kv)


def _load_rounded(srcs, dsts, stages, sems):
    n_slots = stages[0].shape[0]
    n = srcs[0].shape[0] // stages[0].shape[1]
    assert all(s.shape[0] == n * st.shape[1] and st.shape[0] == n_slots for s, st in zip(srcs, stages))

    def rows_of(i, k):
        r = stages[i].shape[1]
        return pl.ds(k * r, r)

    def copy(i, k):
        return pltpu.make_async_copy(srcs[i].at[rows_of(i, k), :], stages[i].at[k % n_slots],
                                     sems.at[i, k % n_slots])

    for k in range(n_slots - 1):
        for i in range(len(srcs)):
            copy(i, k).start()
    for k in range(n):
        for i in range(len(srcs)):
            if k + n_slots - 1 < n:
                copy(i, k + n_slots - 1).start()
            copy(i, k).wait()
            dsts[i][rows_of(i, k), :] = stages[i][k % n_slots].astype(BF16)


def _mixer_kernel(x_ref, kv_ref, g_mix_ref, w_in_hbm, ln_g_ref, ln_b_ref, w_s_ref,
                  bias_ref, conv_w_ref, g_head_ref, w_o_hbm, w1_ref, w2_ref,
                  o_ref, w1b_ref, w2b_ref,
                  w_in_ref, w_o_ref, stage_in, stage_o, sems,
                  heads_a_ref, heads_b_ref, heads_c_ref, carry_ref, *, ts, d_sgu, d_conv, d_mem):
    s = pl.program_id(1)

    @pl.when(jnp.logical_and(pl.program_id(0) == 0, s == 0))
    def _():
        _load_rounded([w_in_hbm, w_o_hbm], [w_in_ref, w_o_ref], [stage_in, stage_o], sems)

    n_chunks = ts // CHUNK
    n_sgu_heads = d_sgu // HEAD_DIM

    @pl.when(s == 0)
    def _():
        carry_ref[...] = jnp.zeros_like(carry_ref)

    x = x_ref[...]
    h = _rms(x, g_mix_ref[...]).astype(BF16)
    o_conv = 2 * d_sgu
    o_q = o_conv + 3 * d_conv

    def head_norm(a, col):
        y = a * lax.rsqrt(jnp.mean(a * a, axis=-1, keepdims=True) + EPS)
        return (y * g_head_ref[:, col:col + HEAD_DIM]).astype(BF16)

    uv = _dot(h, w_in_ref[:, 0:2 * d_sgu])
    q = _dot(h, w_in_ref[:, o_q:o_q + d_mem]) * (HEAD_DIM ** -0.5)
    cx = _dot(h, w_in_ref[:, o_conv + d_conv:o_conv + 3 * d_conv])

    u = jax.nn.gelu(uv[:, 0:d_sgu])
    v = jax.nn.gelu(uv[:, d_sgu:2 * d_sgu])
    mu = jnp.mean(v, axis=-1, keepdims=True)
    vc = v - mu
    vn = vc * lax.rsqrt(jnp.mean(vc * vc, axis=-1, keepdims=True) + EPS)
    vn = (vn * ln_g_ref[...] + ln_b_ref[...]).astype(BF16)

    scores = []
    for hd in range(N_MEM_HEADS):
        col = hd * HEAD_DIM
        q_h = q[:, col:col + HEAD_DIM].astype(BF16)
        k_h = kv_ref[:, col:col + HEAD_DIM]
        scores.append(lax.dot_general(q_h, k_h, (((1,), (1,)), ((), ())),
                                      preferred_element_type=F32))
    b_gate = _dot(h, w_in_ref[:, o_conv:o_conv + d_conv])

    w1b_ref[...] = w1_ref[...].astype(BF16)
    w2b_ref[...] = w2_ref[...].astype(BF16)

    t_idx = lax.broadcasted_iota(jnp.int32, (CHUNK, CHUNK), 0)
    s_idx = lax.broadcasted_iota(jnp.int32, (CHUNK, CHUNK), 1)
    causal = s_idx <= t_idx
    mixed = []
    for hd in range(n_sgu_heads):
        col = hd * HEAD_DIM
        w = jnp.where(causal, w_s_ref[hd], 0.0).astype(BF16)
        v_h = jnp.concatenate(
            [vn[c * CHUNK:(c + 1) * CHUNK, col:col + HEAD_DIM] for c in range(n_chunks)], axis=1)
        mixed.append(_dot(w, v_h))

    pv = []
    for hd in range(N_MEM_HEADS):
        col = hd * HEAD_DIM
        sc = scores[hd]
        e = jnp.exp(sc - jnp.max(sc, axis=-1, keepdims=True))
        denom = jnp.sum(e, axis=-1, keepdims=True)
        v_h = kv_ref[:, d_mem + col:d_mem + col + HEAD_DIM]
        pv.append((_dot(e.astype(BF16), v_h), denom))

    xc = cx[:, 0:d_conv] * cx[:, d_conv:2 * d_conv]
    prev = carry_ref[...]
    prev_m1 = prev[F32_SUBLANES - 1:F32_SUBLANES, :]
    prev_m2 = prev[F32_SUBLANES - 2:F32_SUBLANES - 1, :]
    rows = lax.broadcasted_iota(jnp.int32, (ts, d_conv), 0)
    x_m1 = jnp.where(rows == 0, prev_m1, pltpu.roll(xc, 1, 0))
    x_m2 = jnp.where(rows == 0, prev_m2, jnp.where(rows == 1, prev_m1, pltpu.roll(xc, 2, 0)))
    y = (conv_w_ref[0:1, :] * x_m2 + conv_w_ref[1:2, :] * x_m1 + conv_w_ref[2:3, :] * xc)
    c_out = b_gate * y
    carry_ref[...] = xc[ts - F32_SUBLANES:ts, :]
    for hd in range(d_conv // HEAD_DIM):
        col = hd * HEAD_DIM
        heads_b_ref[:, col:col + HEAD_DIM] = head_norm(c_out[:, col:col + HEAD_DIM], d_sgu + col)
    out = x + _dot(heads_b_ref[...], w_o_ref[d_sgu:d_sgu + d_conv, :])

    for hd in range(n_sgu_heads):
        col = hd * HEAD_DIM
        bias = bias_ref[:, col:col + HEAD_DIM]
        for c in range(n_chunks):
            r0 = c * CHUNK
            a = u[r0:r0 + CHUNK, col:col + HEAD_DIM] * (
                mixed[hd][:, c * HEAD_DIM:(c + 1) * HEAD_DIM] + bias)
            heads_a_ref[r0:r0 + CHUNK, col:col + HEAD_DIM] = head_norm(a, col)
    out = out + _dot(heads_a_ref[...], w_o_ref[0:d_sgu, :])

    for hd in range(N_MEM_HEADS):
        col = hd * HEAD_DIM
        o_h, denom = pv[hd]
        heads_c_ref[:, col:col + HEAD_DIM] = head_norm(o_h / denom, d_sgu + d_conv + col)
    out = out + _dot(heads_c_ref[...], w_o_ref[d_sgu + d_conv:d_sgu + d_conv + d_mem, :])
    o_ref[...] = out


def _mixer_call(x, kv, g_mix, w_in, ln_g, ln_b, w_s, bias_full, conv_w, g_head, w_o, w1, w2):
    bsz, seq, d = x.shape
    d_in = w_in.shape[1]
    d_mix = w_o.shape[0]
    d_sgu = ln_g.shape[1]
    d_conv = conv_w.shape[1]
    d_mem = kv.shape[2] // 2
    n_mem = kv.shape[1]
    ts = SEQ_TILE
    assert seq % ts == 0 and ts % CHUNK == 0
    assert d_in == 2 * d_sgu + 3 * d_conv + d_mem and d_mix == d_sgu + d_conv + d_mem
    assert d % (BF16_SUBLANES * WEIGHT_CHUNKS) == 0 and d_mix % (BF16_SUBLANES * WEIGHT_CHUNKS) == 0
    n_s = seq // ts
    n_steps = bsz * n_s
    assert w1.shape[0] % (BF16_SUBLANES * n_steps) == 0 and w2.shape[0] % (BF16_SUBLANES * n_steps) == 0
    r1, r2 = w1.shape[0] // n_steps, w2.shape[0] // n_steps
    slab = lambda b, s: (b * n_s + s, 0)
    const = lambda b, s: (0, 0)
    kern = functools.partial(_mixer_kernel, ts=ts, d_sgu=d_sgu, d_conv=d_conv, d_mem=d_mem)
    return pl.pallas_call(
        kern,
        grid=(bsz, seq // ts),
        in_specs=[
            pl.BlockSpec((None, ts, d), lambda b, s: (b, s, 0)),
            pl.BlockSpec((None, n_mem, 2 * d_mem), lambda b, s: (b, 0, 0)),
            pl.BlockSpec((1, d), const),
            pl.BlockSpec(memory_space=pl.ANY),
            pl.BlockSpec((1, d_sgu), const),
            pl.BlockSpec((1, d_sgu), const),
            pl.BlockSpec(w_s.shape, lambda b, s: (0, 0, 0)),
            pl.BlockSpec((CHUNK, d_sgu), const),
            pl.BlockSpec((CONV_W, d_conv), const),
            pl.BlockSpec((1, d_mix), const),
            pl.BlockSpec(memory_space=pl.ANY),
            pl.BlockSpec((r1, w1.shape[1]), slab),
            pl.BlockSpec((r2, w2.shape[1]), slab),
        ],
        out_specs=[pl.BlockSpec((None, ts, d), lambda b, s: (b, s, 0)),
                   pl.BlockSpec((r1, w1.shape[1]), slab),
                   pl.BlockSpec((r2, w2.shape[1]), slab)],
        out_shape=[jax.ShapeDtypeStruct((bsz, seq, d), F32),
                   jax.ShapeDtypeStruct(w1.shape, BF16),
                   jax.ShapeDtypeStruct(w2.shape, BF16)],
        scratch_shapes=[
            pltpu.VMEM((d, d_in), BF16),
            pltpu.VMEM((d_mix, d), BF16),
            pltpu.VMEM((WEIGHT_STAGE_SLOTS, d // WEIGHT_CHUNKS, d_in), F32),
            pltpu.VMEM((WEIGHT_STAGE_SLOTS, d_mix // WEIGHT_CHUNKS, d), F32),
            pltpu.SemaphoreType.DMA((2, WEIGHT_STAGE_SLOTS)),
            pltpu.VMEM((ts, d_sgu), BF16),
            pltpu.VMEM((ts, d_conv), BF16),
            pltpu.VMEM((ts, d_mem), BF16),
            pltpu.VMEM((F32_SUBLANES, d_conv), F32),
        ],
        compiler_params=pltpu.CompilerParams(
            dimension_semantics=("arbitrary", "arbitrary"), vmem_limit_bytes=VMEM_LIMIT_BYTES),
        name="mixer",
    )(x, kv, g_mix, w_in, ln_g, ln_b, w_s, bias_full, conv_w, g_head, w_o, w1, w2)


def _ffn_kernel(x_ref, g_ffn_ref, w1_ref, w2_ref, g_final_ref, o_ref, hn_ref):
    j = pl.program_id(1)
    sub = w1_ref.shape[1] // FFN_SUB_TILES

    @pl.when(j == 0)
    def _():
        x = x_ref[...]
        hn_ref[...] = _rms(x, g_ffn_ref[...]).astype(BF16)
        o_ref[...] = x

    hn = hn_ref[...]
    for c in range(FFN_SUB_TILES):
        f = jnp.maximum(_dot(hn, w1_ref[:, c * sub:(c + 1) * sub]), 0.0)
        o_ref[...] += _dot((f * f).astype(BF16), w2_ref[c * sub:(c + 1) * sub, :])

    @pl.when(j == pl.num_programs(1) - 1)
    def _():
        o_ref[...] = _rms(o_ref[...], g_final_ref[...])


def _ffn_call(x, g_ffn, w1, w2, g_final):
    t, d = x.shape
    d_ff = w1.shape[1]
    tm, tf = FFN_ROW_TILE, FFN_COL_TILE
    assert t % tm == 0 and d_ff % tf == 0 and tf % (FFN_SUB_TILES * 2 * MXU_TILE) == 0
    return pl.pallas_call(
        _ffn_kernel,
        grid=(t // tm, d_ff // tf),
        in_specs=[
            pl.BlockSpec((tm, d), lambda i, j: (i, 0)),
            pl.BlockSpec((1, d), lambda i, j: (0, 0)),
            pl.BlockSpec((d, tf), lambda i, j: (0, j)),
            pl.BlockSpec((tf, d), lambda i, j: (j, 0)),
            pl.BlockSpec((1, d), lambda i, j: (0, 0)),
        ],
        out_specs=pl.BlockSpec((tm, d), lambda i, j: (i, 0)),
        out_shape=jax.ShapeDtypeStruct((t, d), F32),
        scratch_shapes=[pltpu.VMEM((tm, d), BF16)],
        compiler_params=pltpu.CompilerParams(
            dimension_semantics=("arbitrary", "arbitrary"), vmem_limit_bytes=VMEM_LIMIT_BYTES),
        name="ffn",
    )(x, g_ffn, w1, w2, g_final)


def kernel(x, mem, g_mix, w_in, ln_v_g, ln_v_b, w_s, b_s, conv_w, g_mem, w_kv, g_head, w_o,
           g_ffn, w_ffn1, w_ffn2, g_final):
    bsz, seq, d = x.shape
    assert w_in.shape[0] == 1, "the FFN call fuses the final norm, so exactly one layer is supported"
    kv = _kv_call(mem, g_mem[0][None, :], w_kv[0])
    bias_full = jnp.repeat(b_s[0].T, HEAD_DIM, axis=1)
    x1, w1b, w2b = _mixer_call(x, kv, g_mix[0][None, :], w_in[0], ln_v_g[0][None, :],
                               ln_v_b[0][None, :], w_s[0], bias_full, conv_w[0], g_head[0][None, :],
                               w_o[0], w_ffn1[0], w_ffn2[0])
    y = _ffn_call(x1.reshape(bsz * seq, d), g_ffn[0][None, :], w1b, w2b, g_final[None, :])
    return y.reshape(bsz, seq, d)
```

```python
import functools

import jax
import jax.numpy as jnp
from jax import lax
from jax.experimental import pallas as pl
from jax.experimental.pallas import tpu as pltpu

F32_SUBLANES = 8
BF16_SUBLANES = 16
MXU_TILE = 256
VMEM_LIMIT_BYTES = 58 * 1024 * 1024

HEAD_DIM = 128
N_MEM_HEADS = 4
CHUNK = 128
CONV_W = 3
EPS = 1e-6

SEQ_TILE = 256
FFN_ROW_TILE = 512
FFN_COL_TILE = 2048
FFN_SUB_TILES = 1
WEIGHT_CHUNKS = 32
WEIGHT_STAGE_SLOTS = 4

F32 = jnp.float32
BF16 = jnp.bfloat16


def _rms(xf, g):
    return xf * lax.rsqrt(jnp.mean(xf * xf, axis=-1, keepdims=True) + EPS) * g


_dot = functools.partial(jnp.dot, preferred_element_type=F32)


def _kv_kernel(mem_ref, g_ref, w_ref, kv_ref, w_b_ref):
    @pl.when(pl.program_id(0) == 0)
    def _():
        w_b_ref[...] = w_ref[...].astype(BF16)

    mem = mem_ref[...]
    r = lax.rsqrt(jnp.mean(mem * mem, axis=-1, keepdims=True) + EPS)
    kv_ref[...] = (_dot((mem * g_ref[...]).astype(BF16), w_b_ref[...]) * r).astype(BF16)


def _kv_call(mem, g_mem, w_kv):
    bsz, n_mem, d = mem.shape
    n_out = w_kv.shape[1]
    return pl.pallas_call(
        _kv_kernel,
        grid=(bsz,),
        in_specs=[
            pl.BlockSpec((None, n_mem, d), lambda b: (b, 0, 0)),
            pl.BlockSpec((1, d), lambda b: (0, 0)),
            pl.BlockSpec((d, n_out), lambda b: (0, 0), pipeline_mode=pl.Buffered(1)),
        ],
        out_specs=pl.BlockSpec((None, n_mem, n_out), lambda b: (b, 0, 0)),
        out_shape=jax.ShapeDtypeStruct((bsz, n_mem, n_out), BF16),
        scratch_shapes=[pltpu.VMEM((d, n_out), BF16)],
        compiler_params=pltpu.CompilerParams(
            dimension_semantics=("arbitrary",), vmem_limit_bytes=VMEM_LIMIT_BYTES),
        name="mem_kv",
    )(mem, g_mem, w_kv)


def _load_rounded(srcs, dsts, stages, sems):
    n_slots = stages[0].shape[0]
    n = srcs[0].shape[0] // stages[0].shape[1]
    assert all(s.shape[0] == n * st.shape[1] and st.shape[0] == n_slots for s, st in zip(srcs, stages))

    def rows_of(i, k):
        r = stages[i].shape[1]
        return pl.ds(k * r, r)

    def copy(i, k):
        return pltpu.make_async_copy(srcs[i].at[rows_of(i, k), :], stages[i].at[k % n_slots],
                                     sems.at[i, k % n_slots])

    for k in range(n_slots - 1):
        for i in range(len(srcs)):
            copy(i, k).start()
    for k in range(n):
        for i in range(len(srcs)):
            if k + n_slots - 1 < n:
                copy(i, k + n_slots - 1).start()
            copy(i, k).wait()
            dsts[i][rows_of(i, k), :] = stages[i][k % n_slots].astype(BF16)


def _mixer_kernel(x_ref, kv_ref, g_mix_ref, w_in_hbm, ln_g_ref, ln_b_ref, w_s_ref,
                  bias_ref, conv_w_ref, g_head_ref, w_o_hbm, w1_ref, w2_ref,
                  o_ref, w1b_ref, w2b_ref,
                  w_in_ref, w_o_ref, stage_in, stage_o, sems,
                  heads_a_ref, heads_b_ref, heads_c_ref, carry_ref, *, ts, d_sgu, d_conv, d_mem):
    s = pl.program_id(1)

    @pl.when(jnp.logical_and(pl.program_id(0) == 0, s == 0))
    def _():
        _load_rounded([w_in_hbm, w_o_hbm], [w_in_ref, w_o_ref], [stage_in, stage_o], sems)

    n_chunks = ts // CHUNK
    n_sgu_heads = d_sgu // HEAD_DIM

    @pl.when(s == 0)
    def _():
        carry_ref[...] = jnp.zeros_like(carry_ref)

    x = x_ref[...]
    h = (x * g_mix_ref[...]).astype(BF16)
    r = lax.rsqrt(jnp.mean(x * x, axis=-1, keepdims=True) + EPS)
    o_conv = 2 * d_sgu
    o_q = o_conv + 3 * d_conv

    def head_norm(a, col):
        y = a * lax.rsqrt(jnp.mean(a * a, axis=-1, keepdims=True) + EPS)
        return (y * g_head_ref[:, col:col + HEAD_DIM]).astype(BF16)

    q = _dot(h, w_in_ref[:, o_q:o_q + d_mem]) * (r * (HEAD_DIM ** -0.5))
    uv = _dot(h, w_in_ref[:, 0:2 * d_sgu])
    cx = _dot(h, w_in_ref[:, o_conv + d_conv:o_conv + 3 * d_conv])

    u = jax.nn.gelu(uv[:, 0:d_sgu] * r)
    v = jax.nn.gelu(uv[:, d_sgu:2 * d_sgu] * r)
    mu = jnp.mean(v, axis=-1, keepdims=True)
    vc = v - mu
    vn = vc * lax.rsqrt(jnp.mean(vc * vc, axis=-1, keepdims=True) + EPS)
    vn = (vn * ln_g_ref[...] + ln_b_ref[...]).astype(BF16)

    scores = []
    for hd in range(N_MEM_HEADS):
        col = hd * HEAD_DIM
        q_h = q[:, col:col + HEAD_DIM].astype(BF16)
        k_h = kv_ref[:, col:col + HEAD_DIM]
        scores.append(lax.dot_general(q_h, k_h, (((1,), (1,)), ((), ())),
                                      preferred_element_type=F32))
    b_gate = _dot(h, w_in_ref[:, o_conv:o_conv + d_conv]) * r

    w1b_ref[...] = w1_ref[...].astype(BF16)
    w2b_ref[...] = w2_ref[...].astype(BF16)

    t_idx = lax.broadcasted_iota(jnp.int32, (CHUNK, CHUNK), 0)
    s_idx = lax.broadcasted_iota(jnp.int32, (CHUNK, CHUNK), 1)
    causal = s_idx <= t_idx
    mixed = []
    for hd in range(n_sgu_heads):
        col = hd * HEAD_DIM
        w = jnp.where(causal, w_s_ref[hd], 0.0).astype(BF16)
        v_h = jnp.concatenate(
            [vn[c * CHUNK:(c + 1) * CHUNK, col:col + HEAD_DIM] for c in range(n_chunks)], axis=1)
        mixed.append(_dot(w, v_h))

    pv = []
    for hd in range(N_MEM_HEADS):
        col = hd * HEAD_DIM
        sc = scores[hd]
        e = jnp.exp(sc - jnp.max(sc, axis=-1, keepdims=True))
        denom = jnp.sum(e, axis=-1, keepdims=True)
        v_h = kv_ref[:, d_mem + col:d_mem + col + HEAD_DIM]
        pv.append((_dot(e.astype(BF16), v_h), denom))

    xc = cx[:, 0:d_conv] * cx[:, d_conv:2 * d_conv] * (r * r)
    prev = carry_ref[...]
    prev_m1 = prev[F32_SUBLANES - 1:F32_SUBLANES, :]
    prev_m2 = prev[F32_SUBLANES - 2:F32_SUBLANES - 1, :]
    rows = lax.broadcasted_iota(jnp.int32, (ts, d_conv), 0)
    x_m1 = jnp.where(rows == 0, prev_m1, pltpu.roll(xc, 1, 0))
    x_m2 = jnp.where(rows == 0, prev_m2, jnp.where(rows == 1, prev_m1, pltpu.roll(xc, 2, 0)))
    y = (conv_w_ref[0:1, :] * x_m2 + conv_w_ref[1:2, :] * x_m1 + conv_w_ref[2:3, :] * xc)
    c_out = b_gate * y
    carry_ref[...] = xc[ts - F32_SUBLANES:ts, :]
    for hd in range(d_conv // HEAD_DIM):
        col = hd * HEAD_DIM
        heads_b_ref[:, col:col + HEAD_DIM] = head_norm(c_out[:, col:col + HEAD_DIM], d_sgu + col)
    out = x + _dot(heads_b_ref[...], w_o_ref[d_sgu:d_sgu + d_conv, :])

    for hd in range(n_sgu_heads):
        col = hd * HEAD_DIM
        bias = bias_ref[:, col:col + HEAD_DIM]
        for c in range(n_chunks):
            r0 = c * CHUNK
            a = u[r0:r0 + CHUNK, col:col + HEAD_DIM] * (
                mixed[hd][:, c * HEAD_DIM:(c + 1) * HEAD_DIM] + bias)
            heads_a_ref[r0:r0 + CHUNK, col:col + HEAD_DIM] = head_norm(a, col)
    out = out + _dot(heads_a_ref[...], w_o_ref[0:d_sgu, :])

    for hd in range(N_MEM_HEADS):
        col = hd * HEAD_DIM
        o_h, denom = pv[hd]
        heads_c_ref[:, col:col + HEAD_DIM] = head_norm(o_h / denom, d_sgu + d_conv + col)
    out = out + _dot(heads_c_ref[...], w_o_ref[d_sgu + d_conv:d_sgu + d_conv + d_mem, :])
    o_ref[...] = out


def _mixer_call(x, kv, g_mix, w_in, ln_g, ln_b, w_s, bias_full, conv_w, g_head, w_o, w1, w2):
    bsz, seq, d = x.shape
    d_in = w_in.shape[1]
    d_mix = w_o.shape[0]
    d_sgu = ln_g.shape[1]
    d_conv = conv_w.shape[1]
    d_mem = kv.shape[2] // 2
    n_mem = kv.shape[1]
    ts = SEQ_TILE
    assert seq % ts == 0 and ts % CHUNK == 0
    assert d_in == 2 * d_sgu + 3 * d_conv + d_mem and d_mix == d_sgu + d_conv + d_mem
    assert d % (BF16_SUBLANES * WEIGHT_CHUNKS) == 0 and d_mix % (BF16_SUBLANES * WEIGHT_CHUNKS) == 0
    n_s = seq // ts
    n_steps = bsz * n_s
    assert w1.shape[0] % (BF16_SUBLANES * n_steps) == 0 and w2.shape[0] % (BF16_SUBLANES * n_steps) == 0
    r1, r2 = w1.shape[0] // n_steps, w2.shape[0] // n_steps
    slab = lambda b, s: (b * n_s + s, 0)
    const = lambda b, s: (0, 0)
    kern = functools.partial(_mixer_kernel, ts=ts, d_sgu=d_sgu, d_conv=d_conv, d_mem=d_mem)
    return pl.pallas_call(
        kern,
        grid=(bsz, seq // ts),
        in_specs=[
            pl.BlockSpec((None, ts, d), lambda b, s: (b, s, 0)),
            pl.BlockSpec((None, n_mem, 2 * d_mem), lambda b, s: (b, 0, 0)),
            pl.BlockSpec((1, d), const),
            pl.BlockSpec(memory_space=pl.ANY),
            pl.BlockSpec((1, d_sgu), const),
            pl.BlockSpec((1, d_sgu), const),
            pl.BlockSpec(w_s.shape, lambda b, s: (0, 0, 0)),
            pl.BlockSpec((CHUNK, d_sgu), const),
            pl.BlockSpec((CONV_W, d_conv), const),
            pl.BlockSpec((1, d_mix), const),
            pl.BlockSpec(memory_space=pl.ANY),
            pl.BlockSpec((r1, w1.shape[1]), slab),
            pl.BlockSpec((r2, w2.shape[1]), slab),
        ],
        out_specs=[pl.BlockSpec((None, ts, d), lambda b, s: (b, s, 0)),
                   pl.BlockSpec((r1, w1.shape[1]), slab),
                   pl.BlockSpec((r2, w2.shape[1]), slab)],
        out_shape=[jax.ShapeDtypeStruct((bsz, seq, d), F32),
                   jax.ShapeDtypeStruct(w1.shape, BF16),
                   jax.ShapeDtypeStruct(w2.shape, BF16)],
        scratch_shapes=[
            pltpu.VMEM((d, d_in), BF16),
            pltpu.VMEM((d_mix, d), BF16),
            pltpu.VMEM((WEIGHT_STAGE_SLOTS, d // WEIGHT_CHUNKS, d_in), F32),
            pltpu.VMEM((WEIGHT_STAGE_SLOTS, d_mix // WEIGHT_CHUNKS, d), F32),
            pltpu.SemaphoreType.DMA((2, WEIGHT_STAGE_SLOTS)),
            pltpu.VMEM((ts, d_sgu), BF16),
            pltpu.VMEM((ts, d_conv), BF16),
            pltpu.VMEM((ts, d_mem), BF16),
            pltpu.VMEM((F32_SUBLANES, d_conv), F32),
        ],
        compiler_params=pltpu.CompilerParams(
            dimension_semantics=("arbitrary", "arbitrary"), vmem_limit_bytes=VMEM_LIMIT_BYTES),
        name="mixer",
    )(x, kv, g_mix, w_in, ln_g, ln_b, w_s, bias_full, conv_w, g_head, w_o, w1, w2)


def _ffn_kernel(x_ref, g_ffn_ref, w1_ref, w2_ref, g_final_ref, o_ref, hn_ref):
    j = pl.program_id(1)
    sub = w1_ref.shape[1] // FFN_SUB_TILES

    @pl.when(j == 0)
    def _():
        x = x_ref[...]
        hn_ref[...] = _rms(x, g_ffn_ref[...]).astype(BF16)
        o_ref[...] = x

    hn = hn_ref[...]
    for c in range(FFN_SUB_TILES):
        f = jnp.maximum(_dot(hn, w1_ref[:, c * sub:(c + 1) * sub]), 0.0)
        o_ref[...] += _dot((f * f).astype(BF16), w2_ref[c * sub:(c + 1) * sub, :])

    @pl.when(j == pl.num_programs(1) - 1)
    def _():
        o_ref[...] = _rms(o_ref[...], g_final_ref[...])


def _ffn_call(x, g_ffn, w1, w2, g_final):
    t, d = x.shape
    d_ff = w1.shape[1]
    tm, tf = FFN_ROW_TILE, FFN_COL_TILE
    assert t % tm == 0 and d_ff % tf == 0 and tf % (FFN_SUB_TILES * 2 * MXU_TILE) == 0
    return pl.pallas_call(
        _ffn_kernel,
        grid=(t // tm, d_ff // tf),
        in_specs=[
            pl.BlockSpec((tm, d), lambda i, j: (i, 0)),
            pl.BlockSpec((1, d), lambda i, j: (0, 0)),
            pl.BlockSpec((d, tf), lambda i, j: (0, j)),
            pl.BlockSpec((tf, d), lambda i, j: (j, 0)),
            pl.BlockSpec((1, d), lambda i, j: (0, 0)),
        ],
        out_specs=pl.BlockSpec((tm, d), lambda i, j: (i, 0)),
        out_shape=jax.ShapeDtypeStruct((t, d), F32),
        scratch_shapes=[pltpu.VMEM((tm, d), BF16)],
        compiler_params=pltpu.CompilerParams(
            dimension_semantics=("arbitrary", "arbitrary"), vmem_limit_bytes=VMEM_LIMIT_BYTES),
        name="ffn",
    )(x, g_ffn, w1, w2, g_final)


def kernel(x, mem, g_mix, w_in, ln_v_g, ln_v_b, w_s, b_s, conv_w, g_mem, w_kv, g_head, w_o,
           g_ffn, w_ffn1, w_ffn2, g_final):
    bsz, seq, d = x.shape
    assert w_in.shape[0] == 1, "the FFN call fuses the final norm, so exactly one layer is supported"
    kv = _kv_call(mem, g_mem[0][None, :], w_kv[0])
    bias_full = jnp.repeat(b_s[0].T, HEAD_DIM, axis=1)
    x1, w1b, w2b = _mixer_call(x, kv, g_mix[0][None, :], w_in[0], ln_v_g[0][None, :],
                               ln_v_b[0][None, :], w_s[0], bias_full, conv_w[0], g_head[0][None, :],
                               w_o[0], w_ffn1[0], w_ffn2[0])
    y = _ffn_call(x1.reshape(bsz * seq, d), g_ffn[0][None, :], w1b, w2b, g_final[None, :])
    return y.reshape(bsz, seq, d)
```
